```python
import math
import jax
import jax.numpy as jnp
from jax import lax
import numpy as np


D_MODEL = 1024
BATCH = 8
SEQ = 4096
DEPTH = 4

CHUNK = 64
N_MEM = 256
EPS = 1e-6

A_HEADS = 8
A_HEAD_DIM = 64
A_WIDTH = A_HEADS * A_HEAD_DIM
IDX_HEADS = 4
IDX_DIM = 64
TOPK_MAX = 256
Q_BLOCK = 128

S5_WIDTH = D_MODEL - A_WIDTH
S5_GROUP = 16
S5_GROUPS = S5_WIDTH // S5_GROUP
S5_STATE = 64

CONV_WIDTH = 3

X_HEADS = 4
X_HEAD_DIM = 128
X_WIDTH = X_HEADS * X_HEAD_DIM

D_FF = 2816

N_EVEN = (DEPTH + 1) // 2
N_ODD = DEPTH // 2

EVEN_SPLIT_SIZES = (A_WIDTH, A_HEAD_DIM, A_HEAD_DIM, IDX_HEADS * IDX_DIM, IDX_DIM, IDX_HEADS, S5_WIDTH)
EVEN_IN = A_WIDTH + 2 * A_HEAD_DIM + IDX_HEADS * IDX_DIM + IDX_DIM + IDX_HEADS + S5_WIDTH

kernel_name = "hybrid_dsa_s5_shortconv_trunk"


def rms_norm(x, g):
    xf = x.astype(jnp.float32)
    xf = xf * lax.rsqrt(jnp.mean(xf * xf, axis=-1, keepdims=True) + EPS)
    return (xf * g.astype(jnp.float32)).astype(x.dtype)


def causal_dwconv(z, w, b):
    T = z.shape[1]
    zp = jnp.pad(z, ((0, 0), (CONV_WIDTH - 1, 0), (0, 0)))
    y = w[0] * zp[:, 0:T] + w[1] * zp[:, 1:T + 1] + w[2] * zp[:, 2:T + 2]
    return y + b


def dsa_attention(q, k, v, iq, ik, iw):
    B, T = q.shape[0], q.shape[1]
    topk = min(TOPK_MAX, T // 4)
    nb = T // Q_BLOCK
    key_pos = jnp.arange(T)
    idx_scale = (IDX_DIM ** -0.5) * (IDX_HEADS ** -0.5)
    att_scale = A_HEAD_DIM ** -0.5

    def to_blocks(a):
        return jnp.moveaxis(a.reshape((B, nb, Q_BLOCK) + a.shape[2:]), 1, 0)

    def block(args):
        blk, qb, iqb, iwb = args
        t = blk * Q_BLOCK + jnp.arange(Q_BLOCK)
        limit = (t // CHUNK + 1) * CHUNK
        adm = key_pos[None, :] < limit[:, None]
        rel = jax.nn.relu(jnp.einsum('bqhd,bsd->bqhs', iqb, ik))
        score = jnp.einsum('bqh,bqhs->bqs', iwb, rel) * idx_scale
        score = jnp.where(adm[None], score.astype(jnp.float32), -jnp.inf)
        _, sel = lax.top_k(score, topk)
        valid = sel < limit[None, :, None]
        kg = jax.vmap(lambda kk, ii: kk[ii])(k, sel)
        vg = jax.vmap(lambda vv, ii: vv[ii])(v, sel)
        s = jnp.einsum('bqhd,bqkd->bhqk', qb, kg).astype(jnp.float32) * att_scale
        s = jnp.where(valid[:, None], s, -jnp.inf)
        p = jax.nn.softmax(s, axis=-1).astype(v.dtype)
        return jnp.einsum('bhqk,bqkd->bqhd', p, vg)

    out = lax.map(block, (jnp.arange(nb), to_blocks(q), to_blocks(iq), to_blocks(iw)))
    return jnp.moveaxis(out, 0, 1).reshape(B, T, A_WIDTH)


def s5_mixer(u, lam_re, lam_im, log_step, b_re, b_im, c_re, c_im, d, w_glu, b_glu):
    B, T, _ = u.shape
    f32 = jnp.float32
    uf = u.astype(f32).reshape(B, T, S5_GROUPS, S5_GROUP)
    lam = lax.complex(lam_re.astype(f32), lam_im.astype(f32))
    delta = jnp.exp(log_step.astype(f32))[:, None]
    lam_bar = jnp.exp(lam * delta)
    b_bar = ((lam_bar - 1.0) / lam)[..., None] * lax.complex(b_re.astype(f32), b_im.astype(f32))
    bu = jnp.einsum('gpc,btgc->btgp', b_bar, uf.astype(jnp.complex64))
    a = jnp.broadcast_to(lam_bar, (1, T) + lam_bar.shape)

    def combine(e1, e2):
        a1, b1 = e1
        a2, b2 = e2
        return a1 * a2, a2 * b1 + b2

    _, h = lax.associative_scan(combine, (a, bu), axis=1)
    c = lax.complex(c_re.astype(f32), c_im.astype(f32))
    y = jnp.einsum('gcp,btgp->btgc', c, h).real + d.astype(f32).reshape(S5_GROUPS, S5_GROUP) * uf
    y = jax.nn.gelu(y.reshape(B, T, S5_WIDTH)).astype(u.dtype)
    return y * jax.nn.sigmoid(y @ w_glu + b_glu)


def even_mixer(h, w_in, w_out, q_norm, k_norm, lam_re, lam_im, log_step, b_re, b_im, c_re, c_im, d, w_glu, b_glu):
    B, T, _ = h.shape
    split_pts = np.cumsum(EVEN_SPLIT_SIZES)[:-1].tolist()
    q, k, v, iq, ik, iw, u = jnp.split(h @ w_in, split_pts, axis=-1)
    q = rms_norm(q.reshape(B, T, A_HEADS, A_HEAD_DIM), q_norm)
    k = rms_norm(k, k_norm)
    ya = dsa_attention(q, k, v, iq.reshape(B, T, IDX_HEADS, IDX_DIM), ik, iw)
    yb = s5_mixer(u, lam_re, lam_im, log_step, b_re, b_im, c_re, c_im, d, w_glu, b_glu)
    return jnp.concatenate([ya, yb], axis=-1) @ w_out


def short_conv_mixer(h, w_in, conv_w, conv_b, w_out):
    gb, gc, z = jnp.split(h @ w_in, 3, axis=-1)
    return (gb * causal_dwconv(gc * z, conv_w, conv_b)) @ w_out


def memory_cross_attention(h, m, w_q, w_kv, w_o, q_norm, k_norm):
    B, T, _ = h.shape
    M = m.shape[1]
    q = rms_norm((h @ w_q).reshape(B, T, X_HEADS, X_HEAD_DIM), q_norm)
    k, v = jnp.split(m @ w_kv, 2, axis=-1)
    k = rms_norm(k.reshape(B, M, X_HEADS, X_HEAD_DIM), k_norm)
    v = v.reshape(B, M, X_HEADS, X_HEAD_DIM)
    s = jnp.einsum('bthd,bmhd->bhtm', q, k).astype(jnp.float32) * (X_HEAD_DIM ** -0.5)
    p = jax.nn.softmax(s, axis=-1).astype(v.dtype)
    o = jnp.einsum('bhtm,bmhd->bthd', p, v).reshape(B, T, X_WIDTH)
    return o @ w_o


def conv_glu_ffn(h, w_up, conv_w, conv_b, w_down):
    g, u = jnp.split(h @ w_up, 2, axis=-1)
    return (jax.nn.silu(causal_dwconv(g, conv_w, conv_b)) * u) @ w_down


def setup_inputs(seed: int = 0) -> dict:
    key = jax.random.key(seed)
    ks = iter(jax.random.split(key, 48))
    f32 = jnp.float32

    def nrm(shape, fan_in):
        return jax.random.normal(next(ks), shape, f32) * (fan_in ** -0.5)

    def gain(shape):
        return 1.0 + 0.05 * jax.random.normal(next(ks), shape, f32)

    def small(shape):
        return 0.01 * jax.random.normal(next(ks), shape, f32)

    D = D_MODEL
    lam_im_base = math.pi * jnp.arange(S5_STATE, dtype=f32)
    return {
        "x": jax.random.normal(next(ks), (BATCH, SEQ, D), f32),
        "mem": jax.random.normal(next(ks), (BATCH, N_MEM, D), f32),
        "norm_mix": gain((DEPTH, D)),
        "norm_x": gain((DEPTH, D)),
        "norm_mem": gain((DEPTH, D)),
        "norm_ffn": gain((DEPTH, D)),
        "even_w_in": nrm((N_EVEN, D, EVEN_IN), D),
        "even_w_out": nrm((N_EVEN, D, D), D),
        "a_q_norm": gain((N_EVEN, A_HEAD_DIM)),
        "a_k_norm": gain((N_EVEN, A_HEAD_DIM)),
        "s5_lam_re": -0.5 + small((N_EVEN, S5_GROUPS, S5_STATE)),
        "s5_lam_im": lam_im_base + small((N_EVEN, S5_GROUPS, S5_STATE)),
        "s5_log_step": jax.random.uniform(next(ks), (N_EVEN, S5_GROUPS), f32, math.log(1e-3), math.log(1e-1)),
        "s5_b_re": nrm((N_EVEN, S5_GROUPS, S5_STATE, S5_GROUP), 2 * S5_GROUP),
        "s5_b_im": nrm((N_EVEN, S5_GROUPS, S5_STATE, S5_GROUP), 2 * S5_GROUP),
        "s5_c_re": nrm((N_EVEN, S5_GROUPS, S5_GROUP, S5_STATE), S5_STATE),
        "s5_c_im": nrm((N_EVEN, S5_GROUPS, S5_GROUP, S5_STATE), S5_STATE),
        "s5_d": jax.random.normal(next(ks), (N_EVEN, S5_WIDTH), f32),
        "s5_w_glu": nrm((N_EVEN, S5_WIDTH, S5_WIDTH), S5_WIDTH),
        "s5_b_glu": small((N_EVEN, S5_WIDTH)),
        "odd_w_in": nrm((N_ODD, D, 3 * D), D),
        "odd_conv_w": nrm((N_ODD, CONV_WIDTH, D), CONV_WIDTH),
        "odd_conv_b": small((N_ODD, D)),
        "odd_w_out": nrm((N_ODD, D, D), D),
        "x_w_q": nrm((DEPTH, D, X_WIDTH), D),
        "x_w_kv": nrm((DEPTH, D, 2 * X_WIDTH), D),
        "x_w_o": nrm((DEPTH, X_WIDTH, D), X_WIDTH),
        "x_q_norm": gain((DEPTH, X_HEAD_DIM)),
        "x_k_norm": gain((DEPTH, X_HEAD_DIM)),
        "f_w_up": nrm((DEPTH, D, 2 * D_FF), D),
        "f_conv_w": nrm((DEPTH, CONV_WIDTH, D_FF), CONV_WIDTH),
        "f_conv_b": small((DEPTH, D_FF)),
        "f_w_down": nrm((DEPTH, D_FF, D), D_FF),
    }


def reference(x, mem, norm_mix, norm_x, norm_mem, norm_ffn, even_w_in, even_w_out, a_q_norm, a_k_norm,
              s5_lam_re, s5_lam_im, s5_log_step, s5_b_re, s5_b_im, s5_c_re, s5_c_im, s5_d, s5_w_glu, s5_b_glu,
              odd_w_in, odd_conv_w, odd_conv_b, odd_w_out, x_w_q, x_w_kv, x_w_o, x_q_norm, x_k_norm,
              f_w_up, f_conv_w, f_conv_b, f_w_down):
    for i in range(DEPTH):
        j = i // 2
        h = rms_norm(x, norm_mix[i])
        if i % 2 == 0:
            x = x + even_mixer(h, even_w_in[j], even_w_out[j], a_q_norm[j], a_k_norm[j],
                               s5_lam_re[j], s5_lam_im[j], s5_log_step[j], s5_b_re[j], s5_b_im[j],
                               s5_c_re[j], s5_c_im[j], s5_d[j], s5_w_glu[j], s5_b_glu[j])
        else:
            x = x + short_conv_mixer(h, odd_w_in[j], odd_conv_w[j], odd_conv_b[j], odd_w_out[j])
        x = x + memory_cross_attention(rms_norm(x, norm_x[i]), rms_norm(mem, norm_mem[i]),
                                       x_w_q[i], x_w_kv[i], x_w_o[i], x_q_norm[i], x_k_norm[i])
        x = x + conv_glu_ffn(rms_norm(x, norm_ffn[i]), f_w_up[i], f_conv_w[i], f_conv_b[i], f_w_down[i])
    return x
```

```python
import functools
import math

import jax
import jax.numpy as jnp
from jax import lax
from jax.experimental import pallas as pl
from jax.experimental.pallas import tpu as pltpu

F32 = jnp.float32
BF16 = jnp.bfloat16
I32 = jnp.int32

EPS = 1e-6
CHUNK = 64
Q_BLOCK = 128
KEY_TILE = 128
A_HEADS = 8
A_HEAD_DIM = 64
A_WIDTH = A_HEADS * A_HEAD_DIM
IDX_HEADS = 4
IDX_DIM = 64
TOPK_MAX = 256
S5_GROUP = 16
S5_STATE = 64
S5_GROUPS_PER_SLAB = 8
X_HEADS = 4
X_HEAD_DIM = 128
X_WIDTH = X_HEADS * X_HEAD_DIM
CONV_WIDTH = 3
CONV_HALO = 8

V7X_VMEM_LIMIT_BYTES = 56 * 1024 * 1024
NEG_INF = float("-inf")
INT_MIN = -(2 ** 31)


def _cparams(n_axes):
    return pltpu.CompilerParams(
        dimension_semantics=("arbitrary",) * n_axes,
        vmem_limit_bytes=V7X_VMEM_LIMIT_BYTES)


def _rms_rows(xf, g):
    ms = jnp.mean(xf * xf, axis=-1, keepdims=True)
    return xf * lax.rsqrt(ms + EPS) * g


def _dot(a, b):
    return jnp.dot(a, b, preferred_element_type=F32)


def _dot_nt(a, b):
    return lax.dot_general(a, b, (((1,), (1,)), ((), ())), preferred_element_type=F32)


def _resident(shape):
    nd = len(shape)
    return pl.BlockSpec(shape, lambda *_: (0,) * nd)


def _even_in_kernel(x_ref, g_ref, w_ref, q_ref, kv_ref, iq_ref, ikw_ref, u_ref):
    hn = _rms_rows(x_ref[...], g_ref[...])
    y = _dot(hn.astype(BF16), w_ref[...])
    q_ref[...] = y[:, 0:512].astype(BF16)
    kv_ref[...] = y[:, 512:640].astype(BF16)
    iq_ref[...] = y[:, 640:896].astype(BF16)
    ikw_ref[...] = y[:, 896:1024]
    u_ref[...] = y[:, 1024:1536].astype(BF16)


def _even_in(x2, g, w, B, T, tm):
    N, D = x2.shape
    nt = T // tm
    row = lambda b, t: (b * nt + t, 0)
    return pl.pallas_call(
        _even_in_kernel,
        grid=(B, nt),
        in_specs=[pl.BlockSpec((tm, D), row), _resident((1, D)), _resident(w.shape)],
        out_specs=[pl.BlockSpec((tm, 512), row), pl.BlockSpec((tm, 128), row),
                   pl.BlockSpec((tm, 256), row), pl.BlockSpec((tm, 128), row),
                   pl.BlockSpec((tm, 512), lambda b, t: (t, b))],
        out_shape=[jax.ShapeDtypeStruct((N, 512), BF16), jax.ShapeDtypeStruct((N, 128), BF16),
                   jax.ShapeDtypeStruct((N, 256), BF16), jax.ShapeDtypeStruct((N, 128), F32),
                   jax.ShapeDtypeStruct((T, B * 512), BF16)],
        compiler_params=_cparams(2),
        name="even_in",
    )(x2, g, w)


def _dsa_kernel(q_ref, iq_ref, iwq_ref, kv_ref, ikf_ref, qg_ref, kg_ref, bd_ref, ut_ref, o_ref,
                k2_s, ik2_s, vt_s, qs_s, iqs_s, wt_s, key_s, bias_s, m_s, l_s, acc_s, *, topk):
    T = kv_ref.shape[0]
    qb = pl.program_id(1)
    nc = qb + 1

    lane = lax.broadcasted_iota(I32, (KEY_TILE, Q_BLOCK), 1)
    row = lax.broadcasted_iota(I32, (KEY_TILE, Q_BLOCK), 0)
    lo_half = lane < 64

    @pl.when(qb == 0)
    def _prepare_batch():
        def key_tile(c, _):
            off = pl.multiple_of(c * KEY_TILE, KEY_TILE)
            kv = kv_ref[pl.ds(off, KEY_TILE), :].astype(F32)
            k = jnp.where(lo_half, kv, 0.0)
            ms = jnp.sum(k * k, axis=-1, keepdims=True) * (1.0 / A_HEAD_DIM)
            kn = k * lax.rsqrt(ms + EPS) * kg_ref[...]
            k2_s[pl.ds(off, KEY_TILE), :] = (kn + pltpu.roll(kn, 64, 1)).astype(BF16)
            ik = jnp.where(lo_half, ikf_ref[pl.ds(off, KEY_TILE), :], 0.0)
            ik2_s[pl.ds(off, KEY_TILE), :] = (ik + pltpu.roll(ik, 64, 1)).astype(BF16)
            v = pltpu.roll(jnp.where(lo_half, 0.0, kv), 64, 1)
            vt_s[c] = v.T[0:A_HEAD_DIM, :].astype(BF16)
            return 0

        lax.fori_loop(0, T // KEY_TILE, key_tile, 0)

    q = q_ref[...].astype(F32)
    sq = q * q
    sq_hi = sq.astype(BF16)
    sq_lo = (sq - sq_hi.astype(F32)).astype(BF16)
    ms = _dot(sq_hi, bd_ref[...]) + _dot(sq_lo, bd_ref[...])
    qn = q * lax.rsqrt(ms + EPS) * qg_ref[...] * (A_HEAD_DIM ** -0.5)
    for h in range(A_HEADS):
        blk = qn[:, 128 * (h // 2):128 * (h // 2) + 128]
        keep = lo_half if h % 2 == 0 else jnp.logical_not(lo_half)
        qs_s[h * Q_BLOCK:(h + 1) * Q_BLOCK, :] = jnp.where(keep, blk, 0.0).astype(BF16)
    iq = iq_ref[...].astype(F32)
    for h in range(IDX_HEADS):
        blk = iq[:, 128 * (h // 2):128 * (h // 2) + 128]
        keep = lo_half if h % 2 == 0 else jnp.logical_not(lo_half)
        iqs_s[h * Q_BLOCK:(h + 1) * Q_BLOCK, :] = jnp.where(keep, blk, 0.0).astype(BF16)
    wt_s[...] = iwq_ref[...].T * ((IDX_DIM ** -0.5) * (IDX_HEADS ** -0.5))

    chunk_shift = CHUNK.bit_length() - 1
    limit = (((qb * Q_BLOCK + lane) >> chunk_shift) + 1) << chunk_shift

    def score_tile(c, _):
        off = pl.multiple_of(c * KEY_TILE, KEY_TILE)
        x = _dot_nt(ik2_s[pl.ds(off, KEY_TILE), :], iqs_s[...])
        sc = wt_s[64:65, :] * jnp.maximum(x[:, 0:128], 0.0)
        for h in range(1, IDX_HEADS):
            sc = sc + wt_s[64 + h:65 + h, :] * jnp.maximum(x[:, 128 * h:128 * h + 128], 0.0)
        sc = jnp.where(off + row < limit, sc, NEG_INF)
        bits = pltpu.bitcast(sc, I32)
        key_s[pl.ds(off, KEY_TILE), :] = bits ^ ((bits >> 31) & 0x7FFFFFFF)
        return 0

    lax.fori_loop(0, nc, score_tile, 0)

    def count_ge(cand):
        def body(c, acc):
            off = pl.multiple_of(c * KEY_TILE, KEY_TILE)
            ge = jnp.where(key_s[pl.ds(off, KEY_TILE), :] >= cand, 1, 0)
            return acc + jnp.sum(ge.reshape(KEY_TILE // 8, 8, Q_BLOCK), axis=0)
        acc = lax.fori_loop(0, nc, body, jnp.zeros((8, Q_BLOCK), I32))
        return jnp.sum(acc, axis=0, keepdims=True)

    def bit_step(i, carry):
        prefix, cnt_ge = carry
        cand_u = prefix | lax.shift_left(jnp.int32(1), 31 - i)
        cnt = count_ge(cand_u ^ INT_MIN)
        ok = cnt >= topk
        return jnp.where(ok, cand_u, prefix), jnp.where(ok, cnt, cnt_ge)

    prefix, cnt_ge = lax.fori_loop(
        0, 32, bit_step,
        (jnp.zeros((1, Q_BLOCK), I32), jnp.zeros((1, Q_BLOCK), I32) + nc * KEY_TILE))
    thresh = prefix ^ INT_MIN
    excess = (cnt_ge - topk).astype(F32)

    def mask_tile(i, later_ties):
        c = nc - 1 - i
        off = pl.multiple_of(c * KEY_TILE, KEY_TILE)
        key = key_s[pl.ds(off, KEY_TILE), :]
        eq = key == thresh
        eq_f = jnp.where(eq, 1.0, 0.0)
        suffix = later_ties + _dot(ut_ref[...], eq_f.astype(BF16))
        tie_bias = jnp.where(suffix > excess, 0.0, NEG_INF)
        bias = jnp.where(key > thresh, 0.0, jnp.where(eq, tie_bias, NEG_INF))
        bias_s[pl.ds(off, KEY_TILE), :] = jnp.where(off + row < limit, bias, NEG_INF)
        return later_ties + jnp.sum(eq_f, axis=0, keepdims=True)

    lax.fori_loop(0, nc, mask_tile, jnp.zeros((1, Q_BLOCK), F32))

    m_s[...] = jnp.full(m_s.shape, NEG_INF, F32)
    l_s[...] = jnp.zeros(l_s.shape, F32)
    acc_s[...] = jnp.zeros(acc_s.shape, F32)

    def attend_tile(c, _):
        off = pl.multiple_of(c * KEY_TILE, KEY_TILE)
        s_all = _dot_nt(k2_s[pl.ds(off, KEY_TILE), :], qs_s[...])
        bias = bias_s[pl.ds(off, KEY_TILE), :]
        ps, alphas = [], []
        for h in range(A_HEADS):
            s = s_all[:, h * Q_BLOCK:(h + 1) * Q_BLOCK] + bias
            m_old = m_s[h:h + 1, :]
            m_new = jnp.maximum(m_old, jnp.max(s, axis=0, keepdims=True))
            m_safe = jnp.where(m_new == NEG_INF, 0.0, m_new)
            alpha = jnp.exp(m_old - m_safe)
            p = jnp.exp(s - m_safe)
            l_s[h:h + 1, :] = alpha * l_s[h:h + 1, :] + jnp.sum(p, axis=0, keepdims=True)
            m_s[h:h + 1, :] = m_new
            ps.append(p.astype(BF16))
            alphas.append(alpha)
        pv = _dot(vt_s[c], jnp.concatenate(ps, axis=1))
        for h in range(A_HEADS):
            acc_s[h] = alphas[h] * acc_s[h] + pv[:, h * Q_BLOCK:(h + 1) * Q_BLOCK]
        return 0

    lax.fori_loop(0, nc, attend_tile, 0)

    out_t = jnp.concatenate([acc_s[h] / l_s[h:h + 1, :] for h in range(A_HEADS)], axis=0)
    o_ref[...] = out_t.T.astype(BF16)


def _dsa(q, iq, ikw, kv, qg, kg, bd, ut, B, T):
    N = q.shape[0]
    nqb = T // Q_BLOCK
    topk = min(TOPK_MAX, T // 4)
    qrow = lambda b, j: (b * nqb + j, 0)
    brow = lambda b, j: (b, 0)
    return pl.pallas_call(
        functools.partial(_dsa_kernel, topk=topk),
        grid=(B, nqb),
        in_specs=[pl.BlockSpec((Q_BLOCK, 512), qrow), pl.BlockSpec((Q_BLOCK, 256), qrow),
                  pl.BlockSpec((Q_BLOCK, 128), qrow), pl.BlockSpec((T, 128), brow),
                  pl.BlockSpec((T, 128), brow), _resident((1, 512)), _resident((1, 128)),
                  _resident((512, 512)), _resident((KEY_TILE, KEY_TILE))],
        out_specs=pl.BlockSpec((Q_BLOCK, 512), qrow),
        out_shape=jax.ShapeDtypeStruct((N, 512), BF16),
        scratch_shapes=[
            pltpu.VMEM((T, 128), BF16),
            pltpu.VMEM((T, 128), BF16),
            pltpu.VMEM((T // KEY_TILE, A_HEAD_DIM, KEY_TILE), BF16),
            pltpu.VMEM((A_HEADS * Q_BLOCK, 128), BF16),
            pltpu.VMEM((IDX_HEADS * Q_BLOCK, 128), BF16),
            pltpu.VMEM((128, Q_BLOCK), F32),
            pltpu.VMEM((T, Q_BLOCK), I32),
            pltpu.VMEM((T, Q_BLOCK), F32),
            pltpu.VMEM((A_HEADS, Q_BLOCK), F32),
            pltpu.VMEM((A_HEADS, Q_BLOCK), F32),
            pltpu.VMEM((A_HEADS, A_HEAD_DIM, Q_BLOCK), F32),
        ],
        compiler_params=_cparams(2),
        name="dsa_attention",
    )(q, iq, ikw, kv, ikw, qg, kg, bd, ut)


def _gelu_tanh(x):
    return 0.5 * x * (1.0 + jnp.tanh(math.sqrt(2.0 / math.pi) * (x + 0.044715 * (x * x * x))))


def _s5_kernel(u_ref, bm_ref, cm_ref, are_ref, aim_ref, d_ref, wg_ref, bg_ref, o_ref, buf_s, h_s,
               *, B, steps):
    n_slab = bm_ref.shape[0]
    half = bm_ref.shape[2] // 2
    width = 2 * half

    @pl.when(pl.program_id(0) == 0)
    def _init():
        h_s[...] = jnp.zeros(h_s.shape, F32)

    u = u_ref[...]
    for k in range(n_slab):
        buf_s[:, width * k:width * (k + 1)] = _dot(u[:, 128 * k:128 * (k + 1)], bm_ref[k])

    for k in range(n_slab):
        c0 = width * k
        ar = jnp.broadcast_to(are_ref[:, half * k:half * (k + 1)], (B, half))
        ai = jnp.broadcast_to(aim_ref[:, half * k:half * (k + 1)], (B, half))

        def tstep(t, carry, c0=c0, ar=ar, ai=ai):
            hr, hi = carry
            r0 = pl.multiple_of(t * B, B)
            nr = ar * hr - ai * hi + buf_s[pl.ds(r0, B), c0:c0 + half]
            ni = ar * hi + ai * hr + buf_s[pl.ds(r0, B), c0 + half:c0 + width]
            buf_s[pl.ds(r0, B), c0:c0 + half] = nr
            buf_s[pl.ds(r0, B), c0 + half:c0 + width] = ni
            return nr, ni

        hr, hi = lax.fori_loop(0, steps, tstep,
                               (h_s[:, c0:c0 + half], h_s[:, c0 + half:c0 + width]))
        h_s[:, c0:c0 + half] = hr
        h_s[:, c0 + half:c0 + width] = hi

    y = jnp.concatenate(
        [_dot(buf_s[:, width * k:width * (k + 1)].astype(BF16), cm_ref[k]) for k in range(n_slab)],
        axis=1)
    y = _gelu_tanh(y + d_ref[...] * u.astype(F32))
    z = _dot(y.astype(BF16), wg_ref[...]) + bg_ref[...]
    o_ref[...] = (y * (1.0 / (1.0 + jnp.exp(-z)))).astype(BF16)


def _s5(u_tm, bm, cm, a_re, a_im, d, wg, bg, B, T, steps):
    rows = steps * B
    n_state = bm.shape[0] * bm.shape[2]
    return pl.pallas_call(
        functools.partial(_s5_kernel, B=B, steps=steps),
        grid=(T // steps,),
        in_specs=[pl.BlockSpec((rows, 512), lambda i: (i, 0)), _resident(bm.shape), _resident(cm.shape),
                  _resident(a_re.shape), _resident(a_im.shape), _resident(d.shape),
                  _resident(wg.shape), _resident(bg.shape)],
        out_specs=pl.BlockSpec((rows, 512), lambda i: (i, 0)),
        out_shape=jax.ShapeDtypeStruct((T * B, 512), BF16),
        scratch_shapes=[pltpu.VMEM((rows, n_state), F32),
                        pltpu.VMEM((B, n_state), F32)],
        compiler_params=_cparams(1),
        name="s5_scan",
    )(u_tm, bm, cm, a_re, a_im, d, wg, bg)


def _even_out_kernel(x_ref, ya_ref, yb_ref, wa_ref, wb_ref, o_ref):
    o_ref[...] = x_ref[...] + _dot(ya_ref[...], wa_ref[...]) + _dot(yb_ref[...], wb_ref[...])


def _even_out(x2, ya, yb_t, wa, wb, B, T, tm):
    N, D = x2.shape
    nt = T // tm
    row = lambda b, t: (b * nt + t, 0)
    return pl.pallas_call(
        _even_out_kernel,
        grid=(B, nt),
        in_specs=[pl.BlockSpec((tm, D), row), pl.BlockSpec((tm, 512), row),
                  pl.BlockSpec((tm, 512), lambda b, t: (t, b)), _resident(wa.shape), _resident(wb.shape)],
        out_specs=pl.BlockSpec((tm, D), row),
        out_shape=jax.ShapeDtypeStruct((N, D), F32),
        compiler_params=_cparams(2),
        name="even_out",
    )(x2, ya, yb_t, wa, wb)


def _causal_conv3(c, halo_s, w_ref, b_ref, first_tile):
    tm = c.shape[0]

    @pl.when(first_tile)
    def _zero_halo():
        halo_s[0:CONV_HALO, :] = jnp.zeros((CONV_HALO, c.shape[1]), F32)

    halo_s[CONV_HALO:CONV_HALO + tm, :] = c
    c1 = halo_s[CONV_HALO - 1:CONV_HALO - 1 + tm, :]
    c2 = halo_s[CONV_HALO - 2:CONV_HALO - 2 + tm, :]
    y = w_ref[0:1, :] * c2 + w_ref[1:2, :] * c1 + w_ref[2:3, :] * c + b_ref[...]
    halo_s[0:CONV_HALO, :] = halo_s[tm:tm + CONV_HALO, :]
    return y


def _odd_kernel(x_ref, g_ref, wi_ref, cw_ref, cb_ref, wo_ref, o_ref, halo_s):
    D = x_ref.shape[1]
    x = x_ref[...]
    hn = _rms_rows(x, g_ref[...])
    y = _dot(hn.astype(BF16), wi_ref[...])
    conv = _causal_conv3(y[:, D:2 * D] * y[:, 2 * D:3 * D], halo_s, cw_ref, cb_ref,
                         pl.program_id(1) == 0)
    a = y[:, 0:D] * conv
    o_ref[...] = x + _dot(a.astype(BF16), wo_ref[...])


def _odd_mixer(x2, g, wi, cw, cb, wo, B, T, tm):
    N, D = x2.shape
    nt = T // tm
    row = lambda b, t: (b * nt + t, 0)
    return pl.pallas_call(
        _odd_kernel,
        grid=(B, nt),
        in_specs=[pl.BlockSpec((tm, D), row), _resident((1, D)), _resident(wi.shape),
                  _resident(cw.shape), _resident(cb.shape), _resident(wo.shape)],
        out_specs=pl.BlockSpec((tm, D), row),
        out_shape=jax.ShapeDtypeStruct((N, D), F32),
        scratch_shapes=[pltpu.VMEM((tm + CONV_HALO, D), F32)],
        compiler_params=_cparams(2),
        name="odd_mixer",
    )(x2, g, wi, cw, cb, wo)


def _mem_kv_kernel(m_ref, g_ref, w_ref, kg_ref, k_ref, v_ref):
    hn = _rms_rows(m_ref[...], g_ref[0])
    y = _dot(hn.astype(BF16), w_ref[0])
    ks = []
    for h in range(X_HEADS):
        ks.append(_rms_rows(y[:, X_HEAD_DIM * h:X_HEAD_DIM * (h + 1)], kg_ref[0]))
    k_ref[0] = jnp.concatenate(ks, axis=1).astype(BF16)
    v_ref[0] = y[:, X_WIDTH:2 * X_WIDTH].astype(BF16)


def _mem_kv(mem2, g, w, kg):
    depth = w.shape[0]
    NM, D = mem2.shape
    return pl.pallas_call(
        _mem_kv_kernel,
        grid=(depth,),
        in_specs=[_resident((NM, D)), pl.BlockSpec((1, 1, D), lambda i: (i, 0, 0)),
                  pl.BlockSpec((1, D, 2 * X_WIDTH), lambda i: (i, 0, 0)),
                  pl.BlockSpec((1, 1, X_HEAD_DIM), lambda i: (i, 0, 0))],
        out_specs=[pl.BlockSpec((1, NM, X_WIDTH), lambda i: (i, 0, 0)),
                   pl.BlockSpec((1, NM, X_WIDTH), lambda i: (i, 0, 0))],
        out_shape=[jax.ShapeDtypeStruct((depth, NM, X_WIDTH), BF16)] * 2,
        compiler_params=_cparams(1),
        name="mem_kv",
    )(mem2, g, w, kg)


def _xattn_kernel(x_ref, g_ref, wq_ref, qg_ref, k_ref, v_ref, wo_ref, o_ref):
    x = x_ref[...]
    hn = _rms_rows(x, g_ref[...])
    q = _dot(hn.astype(BF16), wq_ref[...])
    k = k_ref[0]
    v = v_ref[0]
    outs = []
    for h in range(X_HEADS):
        sl = slice(X_HEAD_DIM * h, X_HEAD_DIM * (h + 1))
        qn = _rms_rows(q[:, sl], qg_ref[...]) * (X_HEAD_DIM ** -0.5)
        s = _dot_nt(qn.astype(BF16), k[:, sl])
        p = jnp.exp(s - jnp.max(s, axis=-1, keepdims=True))
        o = _dot(p.astype(BF16), v[:, sl])
        outs.append(o / jnp.sum(p, axis=-1, keepdims=True))
    o_ref[...] = x + _dot(jnp.concatenate(outs, axis=1).astype(BF16), wo_ref[...])


def _xattn(x2, g, wq, qg, k_all, v_all, wo, layer, B, T, tm, M):
    N, D = x2.shape
    nt = T // tm
    row = lambda b, t: (b * nt + t, 0)
    mrow = lambda b, t: (layer * B + b, 0, 0)
    return pl.pallas_call(
        _xattn_kernel,
        grid=(B, nt),
        in_specs=[pl.BlockSpec((tm, D), row), _resident((1, D)), _resident(wq.shape),
                  _resident((1, X_HEAD_DIM)), pl.BlockSpec((1, M, X_WIDTH), mrow),
                  pl.BlockSpec((1, M, X_WIDTH), mrow), _resident(wo.shape)],
        out_specs=pl.BlockSpec((tm, D), row),
        out_shape=jax.ShapeDtypeStruct((N, D), F32),
        compiler_params=_cparams(2),
        name="mem_xattn",
    )(x2, g, wq, qg, k_all, v_all, wo)


def _ffn_kernel(x_ref, g_ref, wu_ref, cw_ref, cb_ref, wd_ref, o_ref, halo_s):
    F = cw_ref.shape[1]
    x = x_ref[...]
    hn = _rms_rows(x, g_ref[...])
    y = _dot(hn.astype(BF16), wu_ref[...])
    conv = _causal_conv3(y[:, 0:F], halo_s, cw_ref, cb_ref, pl.program_id(1) == 0)
    act = conv * (1.0 / (1.0 + jnp.exp(-conv))) * y[:, F:2 * F]
    o_ref[...] = x + _dot(act.astype(BF16), wd_ref[...])


def _ffn(x2, g, wu, cw, cb, wd, B, T, tm):
    N, D = x2.shape
    F = cw.shape[1]
    nt = T // tm
    row = lambda b, t: (b * nt + t, 0)
    return pl.pallas_call(
        _ffn_kernel,
        grid=(B, nt),
        in_specs=[pl.BlockSpec((tm, D), row), _resident((1, D)),
                  pl.BlockSpec(wu.shape, lambda b, t: (0, 0), pipeline_mode=pl.Buffered(1)),
                  _resident(cw.shape), _resident(cb.shape),
                  pl.BlockSpec(wd.shape, lambda b, t: (0, 0), pipeline_mode=pl.Buffered(1))],
        out_specs=pl.BlockSpec((tm, D), row),
        out_shape=jax.ShapeDtypeStruct((N, D), F32),
        scratch_shapes=[pltpu.VMEM((tm + CONV_HALO, F), F32)],
        compiler_params=_cparams(2),
        name="conv_glu_ffn",
    )(x2, g, wu, cw, cb, wd)


def _s5_operators(lam_re, lam_im, log_step, b_re, b_im, c_re, c_im):
    G, P = lam_re.shape
    C = b_re.shape[-1]
    S = S5_GROUPS_PER_SLAB
    n_slab = G // S
    lam = lax.complex(lam_re, lam_im)
    delta = jnp.exp(log_step)[:, None]
    lam_bar = jnp.exp(lam * delta)
    b_bar = ((lam_bar - 1.0) / lam)[..., None] * lax.complex(b_re, b_im)
    eye = jnp.eye(S, dtype=F32)

    def in_blocks(m):
        return jnp.einsum('kgpc,gh->kgchp', m.reshape(n_slab, S, P, C), eye).reshape(n_slab, S * C, S * P)

    def out_blocks(m):
        return jnp.einsum('kgcp,gh->kgphc', m.reshape(n_slab, S, C, P), eye).reshape(n_slab, S * P, S * C)

    bm = jnp.concatenate([in_blocks(jnp.real(b_bar)), in_blocks(jnp.imag(b_bar))], axis=2)
    cm = jnp.concatenate([out_blocks(c_re), -out_blocks(c_im)], axis=1)
    a_re = jnp.real(lam_bar).reshape(1, G * P)
    a_im = jnp.imag(lam_bar).reshape(1, G * P)
    return bm.astype(BF16), cm.astype(BF16), a_re, a_im


def _even_w_in_layout(w):
    n_front = A_WIDTH + 2 * A_HEAD_DIM + IDX_HEADS * IDX_DIM + IDX_DIM + IDX_HEADS
    pad = jnp.zeros((w.shape[0], 128 - IDX_DIM - IDX_HEADS), w.dtype)
    return jnp.concatenate([w[:, :n_front], pad, w[:, n_front:]], axis=1).astype(BF16)


def kernel(x, mem, norm_mix, norm_x, norm_mem, norm_ffn, even_w_in, even_w_out, a_q_norm, a_k_norm,
           s5_lam_re, s5_lam_im, s5_log_step, s5_b_re, s5_b_im, s5_c_re, s5_c_im, s5_d, s5_w_glu, s5_b_glu,
           odd_w_in, odd_conv_w, odd_conv_b, odd_w_out, x_w_q, x_w_kv, x_w_o, x_q_norm, x_k_norm,
           f_w_up, f_conv_w, f_conv_b, f_w_down):
    B, T, D = x.shape
    M = mem.shape[1]
    depth = norm_mix.shape[0]
    assert D == 1024 and T % 512 == 0 and even_w_in.shape[2] == 1476
    tm_proj, tm_fused, s5_steps = 512, 256, 64

    x2 = x.reshape(B * T, D)
    mem2 = mem.reshape(B * M, D)

    k_all, v_all = _mem_kv(mem2, norm_mem.reshape(depth, 1, D), x_w_kv.astype(BF16),
                           x_k_norm.reshape(depth, 1, X_HEAD_DIM))
    k_all = k_all.reshape(depth * B, M, X_WIDTH)
    v_all = v_all.reshape(depth * B, M, X_WIDTH)

    head_mean = jnp.kron(jnp.eye(A_HEADS, dtype=F32),
                         jnp.full((A_HEAD_DIM, A_HEAD_DIM), 1.0 / A_HEAD_DIM, F32)).astype(BF16)
    upper_tri = jnp.triu(jnp.ones((KEY_TILE, KEY_TILE), F32)).astype(BF16)

    for i in range(depth):
        j = i // 2
        g_mix = norm_mix[i].reshape(1, D)
        if i % 2 == 0:
            q, kv, iq, ikw, u_t = _even_in(x2, g_mix, _even_w_in_layout(even_w_in[j]), B, T, tm_proj)
            qg = jnp.tile(a_q_norm[j], A_HEADS).reshape(1, A_WIDTH)
            kg = jnp.concatenate([a_k_norm[j], jnp.zeros((128 - A_HEAD_DIM,), F32)]).reshape(1, 128)
            ya = _dsa(q, iq, ikw, kv, qg, kg, head_mean, upper_tri, B, T)
            bm, cm, a_re, a_im = _s5_operators(s5_lam_re[j], s5_lam_im[j], s5_log_step[j],
                                               s5_b_re[j], s5_b_im[j], s5_c_re[j], s5_c_im[j])
            yb_tm = _s5(u_t.reshape(T * B, 512), bm, cm, a_re, a_im, s5_d[j].reshape(1, 512),
                        s5_w_glu[j].astype(BF16), s5_b_glu[j].reshape(1, 512), B, T, s5_steps)
            w_out = even_w_out[j].astype(BF16)
            x2 = _even_out(x2, ya, yb_tm.reshape(T, B * 512), w_out[:A_WIDTH], w_out[A_WIDTH:],
                           B, T, tm_proj)
        else:
            x2 = _odd_mixer(x2, g_mix, odd_w_in[j].astype(BF16), odd_conv_w[j],
                            odd_conv_b[j].reshape(1, D), odd_w_out[j].astype(BF16), B, T, tm_fused)
        x2 = _xattn(x2, norm_x[i].reshape(1, D), x_w_q[i].astype(BF16), x_q_norm[i].reshape(1, X_HEAD_DIM),
                    k_all, v_all, x_w_o[i].astype(BF16), i, B, T, tm_fused, M)
        x2 = _ffn(x2, norm_ffn[i].reshape(1, D), f_w_up[i].astype(BF16), f_conv_w[i],
                  f_conv_b[i].reshape(1, -1), f_w_down[i].astype(BF16), B, T, tm_fused)
    return x2.reshape(B, T, D)
```

```python
import functools
import math

import jax
import jax.numpy as jnp
from jax import lax
from jax.experimental import pallas as pl
from jax.experimental.pallas import tpu as pltpu

F32 = jnp.float32
BF16 = jnp.bfloat16
I32 = jnp.int32

EPS = 1e-6
CHUNK = 64
Q_BLOCK = 128
KEY_TILE = 128
TILES_PER_SUPER = 4
KEY_SUPER = KEY_TILE * TILES_PER_SUPER
BF16_SUBLANES = 16
V_ROWS = 80
LOG2_E = 1.4426950408889634
SHIFT_SLACK = 1.02
MAX_SAFE_SHIFT = 50.0
A_HEADS = 8
A_HEAD_DIM = 64
A_WIDTH = A_HEADS * A_HEAD_DIM
IDX_HEADS = 4
IDX_DIM = 64
TOPK_MAX = 256
S5_GROUPS_PER_SLAB = 8
X_HEADS = 4
X_HEAD_DIM = 128
X_WIDTH = X_HEADS * X_HEAD_DIM
CONV_HALO = 8

V7X_VMEM_LIMIT_BYTES = 56 * 1024 * 1024
NEG_INF = float("-inf")


def _cparams(n_axes):
    return pltpu.CompilerParams(
        dimension_semantics=("arbitrary",) * n_axes,
        vmem_limit_bytes=V7X_VMEM_LIMIT_BYTES)


def _rms_rows(xf, g):
    ms = jnp.mean(xf * xf, axis=-1, keepdims=True)
    return xf * lax.rsqrt(ms + EPS) * g


def _dot(a, b):
    return jnp.dot(a, b, preferred_element_type=F32)


def _dot_nt(a, b):
    return lax.dot_general(a, b, (((1,), (1,)), ((), ())), preferred_element_type=F32)


def _resident(shape):
    nd = len(shape)
    return pl.BlockSpec(shape, lambda *_: (0,) * nd)


def _even_in_kernel(x_ref, g_ref, w_ref, q_ref, kv_ref, iq_ref, ikw_ref, u_ref):
    hn = _rms_rows(x_ref[...], g_ref[...])
    y = _dot(hn.astype(BF16), w_ref[...])
    q_ref[...] = y[:, 0:512].astype(BF16)
    kv_ref[...] = y[:, 512:640].astype(BF16)
    iq_ref[...] = y[:, 640:896].astype(BF16)
    ikw_ref[...] = y[:, 896:1024]
    u_ref[...] = y[:, 1024:1536].astype(BF16)


def _even_in(x2, g, w, B, T, tm):
    N, D = x2.shape
    nt = T // tm
    row = lambda b, t: (b * nt + t, 0)
    return pl.pallas_call(
        _even_in_kernel,
        grid=(B, nt),
        in_specs=[pl.BlockSpec((tm, D), row), _resident((1, D)), _resident(w.shape)],
        out_specs=[pl.BlockSpec((tm, 512), row), pl.BlockSpec((tm, 128), row),
                   pl.BlockSpec((tm, 256), row), pl.BlockSpec((tm, 128), row),
                   pl.BlockSpec((tm, 512), lambda b, t: (t, b))],
        out_shape=[jax.ShapeDtypeStruct((N, 512), BF16), jax.ShapeDtypeStruct((N, 128), BF16),
                   jax.ShapeDtypeStruct((N, 256), BF16), jax.ShapeDtypeStruct((N, 128), F32),
                   jax.ShapeDtypeStruct((T, B * 512), BF16)],
        compiler_params=_cparams(2),
        name="even_in",
    )(x2, g, w)


def _dsa_kernel(q_ref, iq_ref, iwq_ref, kv_ref, ikf_ref, qg_ref, kg_ref, bd_ref, ut_ref, o_ref,
                k1_s, ik2_s, vt_s, qs_s, iqs_s, wt_s, score_s, bias_s, m_s, acc_s, *, topk):
    T = kv_ref.shape[0]
    qb = pl.program_id(1)
    nst = (qb + TILES_PER_SUPER) >> (TILES_PER_SUPER.bit_length() - 1)

    lane = lax.broadcasted_iota(I32, (KEY_TILE, Q_BLOCK), 1)
    row = lax.broadcasted_iota(I32, (KEY_TILE, Q_BLOCK), 0)
    lo_half = lane < A_HEAD_DIM
    shift_lane = lane == A_HEAD_DIM

    q_scale = (A_HEAD_DIM ** -0.5) * LOG2_E
    shift = (A_HEAD_DIM * q_scale * SHIFT_SLACK) * jnp.max(jnp.abs(qg_ref[...])) * jnp.max(jnp.abs(kg_ref[...]))
    shift_is_safe = shift <= MAX_SAFE_SHIFT

    @pl.when(qb == 0)
    def _prepare_batch():
        ones_rows = jnp.where(lax.broadcasted_iota(I32, (V_ROWS - A_HEAD_DIM, KEY_TILE), 0) == 0, 1.0, 0.0)

        def key_tile(st, _):
            for t in range(TILES_PER_SUPER):
                off = pl.multiple_of(st * KEY_SUPER + t * KEY_TILE, KEY_TILE)
                kv = kv_ref[pl.ds(off, KEY_TILE), :].astype(F32)
                k = jnp.where(lo_half, kv, 0.0)
                ms = jnp.sum(k * k, axis=-1, keepdims=True) * (1.0 / A_HEAD_DIM)
                kn = k * lax.rsqrt(ms + EPS) * kg_ref[...]
                k1_s[pl.ds(off, KEY_TILE), :] = jnp.where(shift_lane, 1.0, kn).astype(BF16)
                ik = jnp.where(lo_half, ikf_ref[pl.ds(off, KEY_TILE), :], 0.0)
                ik2_s[pl.ds(off, KEY_TILE), :] = (ik + pltpu.roll(ik, 64, 1)).astype(BF16)
                v = pltpu.roll(jnp.where(lo_half, 0.0, kv), 64, 1)
                vt1 = jnp.concatenate([v.T[0:A_HEAD_DIM, :], ones_rows], axis=0)
                vt_s[st, :, t * KEY_TILE:(t + 1) * KEY_TILE] = vt1.astype(BF16)
            return 0

        lax.fori_loop(0, T // KEY_SUPER, key_tile, 0)

    q = q_ref[...].astype(F32)
    ms = _dot((q * q).astype(BF16), bd_ref[...])
    qn = q * lax.rsqrt(ms + EPS) * qg_ref[...] * q_scale
    neg_shift_lane = jnp.where(shift_lane, -1.0, 0.0) * shift
    for h in range(A_HEADS):
        blk = qn[:, 128 * (h // 2):128 * (h // 2) + 128]
        if h % 2 == 1:
            blk = pltpu.roll(blk, 64, 1)
        qs_s[h * Q_BLOCK:(h + 1) * Q_BLOCK, :] = (jnp.where(lo_half, blk, 0.0) + neg_shift_lane).astype(BF16)
    iq = iq_ref[...].astype(F32)
    for h in range(IDX_HEADS):
        blk = iq[:, 128 * (h // 2):128 * (h // 2) + 128]
        keep = lo_half if h % 2 == 0 else jnp.logical_not(lo_half)
        iqs_s[h * Q_BLOCK:(h + 1) * Q_BLOCK, :] = jnp.where(keep, blk, 0.0).astype(BF16)
    wt_s[...] = iwq_ref[...].T * ((IDX_DIM ** -0.5) * (IDX_HEADS ** -0.5))

    chunk_shift = CHUNK.bit_length() - 1
    limit = (((qb * Q_BLOCK + lane) >> chunk_shift) + 1) << chunk_shift

    def score_tile(st, _):
        off = pl.multiple_of(st * KEY_SUPER, KEY_SUPER)
        x = _dot_nt(ik2_s[pl.ds(off, KEY_SUPER), :], iqs_s[...])
        for t in range(TILES_PER_SUPER):
            rows = slice(t * KEY_TILE, (t + 1) * KEY_TILE)
            sc = wt_s[64:65, :] * jnp.maximum(x[rows, 0:128], 0.0)
            for h in range(1, IDX_HEADS):
                sc = sc + wt_s[64 + h:65 + h, :] * jnp.maximum(x[rows, 128 * h:128 * h + 128], 0.0)
            sc = jnp.where(off + t * KEY_TILE + row < limit, sc, NEG_INF)
            score_s[pl.ds(off + t * KEY_TILE, KEY_TILE), :] = sc.astype(BF16)
        return 0

    lax.fori_loop(0, nst, score_tile, 0)

    def count_ge(cand):
        def body(st, acc):
            off = pl.multiple_of(st * KEY_SUPER, KEY_SUPER)
            tile = score_s[pl.ds(off, KEY_SUPER), :]
            parts = []
            for t in range(KEY_SUPER // BF16_SUBLANES):
                sc = tile[t * BF16_SUBLANES:(t + 1) * BF16_SUBLANES, :]
                parts.append(jnp.where(sc >= cand, jnp.ones_like(sc), jnp.zeros_like(sc)))
            while len(parts) > 1:
                parts = [parts[i] + parts[i + 1] for i in range(0, len(parts), 2)]
            return acc + parts[0]
        acc = lax.fori_loop(0, nst, body, jnp.zeros((BF16_SUBLANES, Q_BLOCK), BF16))
        return jnp.sum(acc.astype(F32), axis=0, keepdims=True)

    def pattern_to_score(u):
        bits = jnp.where(u >= 0x8000, u ^ 0x8000, u ^ 0xFFFF)
        return jnp.where(u < 0x007F, NEG_INF, pltpu.bitcast(bits << 16, F32))

    def bit_step(i, carry):
        prefix, cnt_ge = carry
        cand_u = prefix | lax.shift_left(jnp.int32(1), 15 - i)
        cand = jnp.broadcast_to(pattern_to_score(cand_u), (BF16_SUBLANES, Q_BLOCK)).astype(BF16)
        cnt = count_ge(cand)
        ok = cnt >= topk
        return jnp.where(ok, cand_u, prefix), jnp.where(ok, cnt, cnt_ge)

    prefix, cnt_ge = lax.fori_loop(
        0, 16, bit_step,
        (jnp.zeros((1, Q_BLOCK), I32), jnp.zeros((1, Q_BLOCK), F32) + (nst * KEY_SUPER).astype(F32)))
    thresh = pattern_to_score(prefix)
    excess = cnt_ge - topk

    def mask_tile(i, later_ties):
        off = pl.multiple_of((nst - 1 - i) * KEY_SUPER, KEY_SUPER)
        for t in reversed(range(TILES_PER_SUPER)):
            o = off + t * KEY_TILE
            sc = score_s[pl.ds(o, KEY_TILE), :].astype(F32)
            eq = sc == thresh
            eq_f = jnp.where(eq, 1.0, 0.0)
            suffix = later_ties + _dot(ut_ref[...], eq_f.astype(BF16))
            tie_bias = jnp.where(suffix > excess, 0.0, NEG_INF)
            bias = jnp.where(sc > thresh, 0.0, jnp.where(eq, tie_bias, NEG_INF))
            bias_s[pl.ds(o, KEY_TILE), :] = jnp.where(o + row < limit, bias, NEG_INF)
            later_ties = later_ties + jnp.sum(eq_f, axis=0, keepdims=True)
        return later_ties

    lax.fori_loop(0, nst, mask_tile, jnp.zeros((1, Q_BLOCK), F32))

    acc_s[...] = jnp.zeros(acc_s.shape, F32)

    @pl.when(shift_is_safe)
    def _attend_shifted():
        def attend_tile(st, _):
            off = pl.multiple_of(st * KEY_SUPER, KEY_SUPER)
            sp = _dot_nt(k1_s[pl.ds(off, KEY_SUPER), :], qs_s[...])
            bias = bias_s[pl.ds(off, KEY_SUPER), :]
            p = jnp.concatenate(
                [jnp.exp2(sp[:, h * Q_BLOCK:(h + 1) * Q_BLOCK] + bias).astype(BF16) for h in range(A_HEADS)],
                axis=1)
            acc_s[...] += _dot(vt_s[st], p)
            return 0

        lax.fori_loop(0, nst, attend_tile, 0)

    @pl.when(jnp.logical_not(shift_is_safe))
    def _attend_online():
        m_s[...] = jnp.full(m_s.shape, NEG_INF, F32)

        def attend_tile(st, _):
            off = pl.multiple_of(st * KEY_SUPER, KEY_SUPER)
            k_tile = k1_s[pl.ds(off, KEY_SUPER), :]
            bias = bias_s[pl.ds(off, KEY_SUPER), :]
            for h in range(A_HEADS):
                cols = slice(h * Q_BLOCK, (h + 1) * Q_BLOCK)
                s = _dot_nt(k_tile, qs_s[cols, :]) + bias
                m_old = m_s[h:h + 1, :]
                m_new = jnp.maximum(m_old, jnp.max(s, axis=0, keepdims=True))
                m_safe = jnp.where(m_new == NEG_INF, 0.0, m_new)
                p = jnp.exp2(s - m_safe).astype(BF16)
                acc_s[:, cols] = jnp.exp2(m_old - m_safe) * acc_s[:, cols] + _dot(vt_s[st], p)
                m_s[h:h + 1, :] = m_new
            return 0

        lax.fori_loop(0, nst, attend_tile, 0)

    out_t = jnp.concatenate(
        [acc_s[0:A_HEAD_DIM, h * Q_BLOCK:(h + 1) * Q_BLOCK] / acc_s[A_HEAD_DIM:A_HEAD_DIM + 1, h * Q_BLOCK:(h + 1) * Q_BLOCK]
         for h in range(A_HEADS)], axis=0)
    o_ref[...] = out_t.T.astype(BF16)


def _dsa(q, iq, ikw, kv, qg, kg, bd, ut, B, T):
    N = q.shape[0]
    nqb = T // Q_BLOCK
    topk = min(TOPK_MAX, T // 4)
    qrow = lambda b, j: (b * nqb + j, 0)
    brow = lambda b, j: (b, 0)
    return pl.pallas_call(
        functools.partial(_dsa_kernel, topk=topk),
        grid=(B, nqb),
        in_specs=[pl.BlockSpec((Q_BLOCK, 512), qrow), pl.BlockSpec((Q_BLOCK, 256), qrow),
                  pl.BlockSpec((Q_BLOCK, 128), qrow), pl.BlockSpec((T, 128), brow),
                  pl.BlockSpec((T, 128), brow), _resident((1, 512)), _resident((1, 128)),
                  _resident((512, 512)), _resident((KEY_TILE, KEY_TILE))],
        out_specs=pl.BlockSpec((Q_BLOCK, 512), qrow),
        out_shape=jax.ShapeDtypeStruct((N, 512), BF16),
        scratch_shapes=[
            pltpu.VMEM((T, 128), BF16),
            pltpu.VMEM((T, 128), BF16),
            pltpu.VMEM((T // KEY_SUPER, V_ROWS, KEY_SUPER), BF16),
            pltpu.VMEM((A_HEADS * Q_BLOCK, 128), BF16),
            pltpu.VMEM((IDX_HEADS * Q_BLOCK, 128), BF16),
            pltpu.VMEM((128, Q_BLOCK), F32),
            pltpu.VMEM((T, Q_BLOCK), BF16),
            pltpu.VMEM((T, Q_BLOCK), F32),
            pltpu.VMEM((A_HEADS, Q_BLOCK), F32),
            pltpu.VMEM((V_ROWS, A_HEADS * Q_BLOCK), F32),
        ],
        compiler_params=_cparams(2),
        name="dsa_attention",
    )(q, iq, ikw, kv, ikw, qg, kg, bd, ut)


def _gelu_tanh(x):
    return 0.5 * x * (1.0 + jnp.tanh(math.sqrt(2.0 / math.pi) * (x + 0.044715 * (x * x * x))))


def _s5_kernel(u_ref, bm_ref, cm_ref, are_ref, aim_ref, d_ref, wg_ref, bg_ref, o_ref, buf_s, h_s,
               *, B, steps):
    n_slab = bm_ref.shape[0]
    half = bm_ref.shape[2] // 2
    width = 2 * half

    @pl.when(pl.program_id(0) == 0)
    def _init():
        h_s[...] = jnp.zeros(h_s.shape, F32)

    u = u_ref[...]
    for k in range(n_slab):
        buf_s[:, width * k:width * (k + 1)] = _dot(u[:, 128 * k:128 * (k + 1)], bm_ref[k])

    for k in range(n_slab):
        c0 = width * k
        ar = jnp.broadcast_to(are_ref[:, half * k:half * (k + 1)], (B, half))
        ai = jnp.broadcast_to(aim_ref[:, half * k:half * (k + 1)], (B, half))

        def tstep(t, carry, c0=c0, ar=ar, ai=ai):
            hr, hi = carry
            r0 = pl.multiple_of(t * B, B)
            nr = ar * hr - ai * hi + buf_s[pl.ds(r0, B), c0:c0 + half]
            ni = ar * hi + ai * hr + buf_s[pl.ds(r0, B), c0 + half:c0 + width]
            buf_s[pl.ds(r0, B), c0:c0 + half] = nr
            buf_s[pl.ds(r0, B), c0 + half:c0 + width] = ni
            return nr, ni

        hr, hi = lax.fori_loop(0, steps, tstep,
                               (h_s[:, c0:c0 + half], h_s[:, c0 + half:c0 + width]))
        h_s[:, c0:c0 + half] = hr
        h_s[:, c0 + half:c0 + width] = hi

    y = jnp.concatenate(
        [_dot(buf_s[:, width * k:width * (k + 1)].astype(BF16), cm_ref[k]) for k in range(n_slab)],
        axis=1)
    y = _gelu_tanh(y + d_ref[...] * u.astype(F32))
    z = _dot(y.astype(BF16), wg_ref[...]) + bg_ref[...]
    o_ref[...] = (y * (1.0 / (1.0 + jnp.exp(-z)))).astype(BF16)


def _s5(u_tm, bm, cm, a_re, a_im, d, wg, bg, B, T, steps):
    rows = steps * B
    n_state = bm.shape[0] * bm.shape[2]
    return pl.pallas_call(
        functools.partial(_s5_kernel, B=B, steps=steps),
        grid=(T // steps,),
        in_specs=[pl.BlockSpec((rows, 512), lambda i: (i, 0)), _resident(bm.shape), _resident(cm.shape),
                  _resident(a_re.shape), _resident(a_im.shape), _resident(d.shape),
                  _resident(wg.shape), _resident(bg.shape)],
        out_specs=pl.BlockSpec((rows, 512), lambda i: (i, 0)),
        out_shape=jax.ShapeDtypeStruct((T * B, 512), BF16),
        scratch_shapes=[pltpu.VMEM((rows, n_state), F32),
                        pltpu.VMEM((B, n_state), F32)],
        compiler_params=_cparams(1),
        name="s5_scan",
    )(u_tm, bm, cm, a_re, a_im, d, wg, bg)


def _even_out_kernel(x_ref, ya_ref, yb_ref, wa_ref, wb_ref, o_ref):
    o_ref[...] = x_ref[...] + _dot(ya_ref[...], wa_ref[...]) + _dot(yb_ref[...], wb_ref[...])


def _even_out(x2, ya, yb_t, wa, wb, B, T, tm):
    N, D = x2.shape
    nt = T // tm
    row = lambda b, t: (b * nt + t, 0)
    return pl.pallas_call(
        _even_out_kernel,
        grid=(B, nt),
        in_specs=[pl.BlockSpec((tm, D), row), pl.BlockSpec((tm, 512), row),
                  pl.BlockSpec((tm, 512), lambda b, t: (t, b)), _resident(wa.shape), _resident(wb.shape)],
        out_specs=pl.BlockSpec((tm, D), row),
        out_shape=jax.ShapeDtypeStruct((N, D), F32),
        compiler_params=_cparams(2),
        name="even_out",
    )(x2, ya, yb_t, wa, wb)


def _causal_conv3(c, halo_s, w_ref, b_ref, first_tile):
    tm = c.shape[0]

    @pl.when(first_tile)
    def _zero_halo():
        halo_s[0:CONV_HALO, :] = jnp.zeros((CONV_HALO, c.shape[1]), F32)

    halo_s[CONV_HALO:CONV_HALO + tm, :] = c
    c1 = halo_s[CONV_HALO - 1:CONV_HALO - 1 + tm, :]
    c2 = halo_s[CONV_HALO - 2:CONV_HALO - 2 + tm, :]
    y = w_ref[0:1, :] * c2 + w_ref[1:2, :] * c1 + w_ref[2:3, :] * c + b_ref[...]
    halo_s[0:CONV_HALO, :] = halo_s[tm:tm + CONV_HALO, :]
    return y


def _odd_kernel(x_ref, g_ref, wi_ref, cw_ref, cb_ref, wo_ref, o_ref, halo_s):
    D = x_ref.shape[1]
    x = x_ref[...]
    hn = _rms_rows(x, g_ref[...])
    y = _dot(hn.astype(BF16), wi_ref[...])
    conv = _causal_conv3(y[:, D:2 * D] * y[:, 2 * D:3 * D], halo_s, cw_ref, cb_ref,
                         pl.program_id(1) == 0)
    a = y[:, 0:D] * conv
    o_ref[...] = x + _dot(a.astype(BF16), wo_ref[...])


def _odd_mixer(x2, g, wi, cw, cb, wo, B, T, tm):
    N, D = x2.shape
    nt = T // tm
    row = lambda b, t: (b * nt + t, 0)
    return pl.pallas_call(
        _odd_kernel,
        grid=(B, nt),
        in_specs=[pl.BlockSpec((tm, D), row), _resident((1, D)), _resident(wi.shape),
                  _resident(cw.shape), _resident(cb.shape), _resident(wo.shape)],
        out_specs=pl.BlockSpec((tm, D), row),
        out_shape=jax.ShapeDtypeStruct((N, D), F32),
        scratch_shapes=[pltpu.VMEM((tm + CONV_HALO, D), F32)],
        compiler_params=_cparams(2),
        name="odd_mixer",
    )(x2, g, wi, cw, cb, wo)


def _mem_kv_kernel(m_ref, g_ref, w_ref, kg_ref, k_ref, v_ref):
    hn = _rms_rows(m_ref[...], g_ref[0])
    y = _dot(hn.astype(BF16), w_ref[0])
    ks = []
    for h in range(X_HEADS):
        ks.append(_rms_rows(y[:, X_HEAD_DIM * h:X_HEAD_DIM * (h + 1)], kg_ref[0]))
    k_ref[0] = jnp.concatenate(ks, axis=1).astype(BF16)
    v_ref[0] = y[:, X_WIDTH:2 * X_WIDTH].astype(BF16)


def _mem_kv(mem2, g, w, kg):
    depth = w.shape[0]
    NM, D = mem2.shape
    return pl.pallas_call(
        _mem_kv_kernel,
        grid=(depth,),
        in_specs=[_resident((NM, D)), pl.BlockSpec((1, 1, D), lambda i: (i, 0, 0)),
                  pl.BlockSpec((1, D, 2 * X_WIDTH), lambda i: (i, 0, 0)),
                  pl.BlockSpec((1, 1, X_HEAD_DIM), lambda i: (i, 0, 0))],
        out_specs=[pl.BlockSpec((1, NM, X_WIDTH), lambda i: (i, 0, 0)),
                   pl.BlockSpec((1, NM, X_WIDTH), lambda i: (i, 0, 0))],
        out_shape=[jax.ShapeDtypeStruct((depth, NM, X_WIDTH), BF16)] * 2,
        compiler_params=_cparams(1),
        name="mem_kv",
    )(mem2, g, w, kg)


def _xattn_kernel(x_ref, g_ref, wq_ref, qg_ref, k_ref, v_ref, wo_ref, o_ref):
    x = x_ref[...]
    hn = _rms_rows(x, g_ref[...])
    q = _dot(hn.astype(BF16), wq_ref[...])
    k = k_ref[0]
    v = v_ref[0]
    outs = []
    for h in range(X_HEADS):
        sl = slice(X_HEAD_DIM * h, X_HEAD_DIM * (h + 1))
        qn = _rms_rows(q[:, sl], qg_ref[...]) * (X_HEAD_DIM ** -0.5)
        s = _dot_nt(qn.astype(BF16), k[:, sl])
        p = jnp.exp(s - jnp.max(s, axis=-1, keepdims=True))
        o = _dot(p.astype(BF16), v[:, sl])
        outs.append(o / jnp.sum(p, axis=-1, keepdims=True))
    o_ref[...] = x + _dot(jnp.concatenate(outs, axis=1).astype(BF16), wo_ref[...])


def _xattn(x2, g, wq, qg, k_all, v_all, wo, layer, B, T, tm, M):
    N, D = x2.shape
    nt = T // tm
    row = lambda b, t: (b * nt + t, 0)
    mrow = lambda b, t: (layer * B + b, 0, 0)
    return pl.pallas_call(
        _xattn_kernel,
        grid=(B, nt),
        in_specs=[pl.BlockSpec((tm, D), row), _resident((1, D)), _resident(wq.shape),
                  _resident((1, X_HEAD_DIM)), pl.BlockSpec((1, M, X_WIDTH), mrow),
                  pl.BlockSpec((1, M, X_WIDTH), mrow), _resident(wo.shape)],
        out_specs=pl.BlockSpec((tm, D), row),
        out_shape=jax.ShapeDtypeStruct((N, D), F32),
        compiler_params=_cparams(2),
        name="mem_xattn",
    )(x2, g, wq, qg, k_all, v_all, wo)


def _ffn_kernel(x_ref, g_ref, wu_ref, cw_ref, cb_ref, wd_ref, o_ref, halo_s):
    F = cw_ref.shape[1]
    x = x_ref[...]
    hn = _rms_rows(x, g_ref[...])
    y = _dot(hn.astype(BF16), wu_ref[...])
    conv = _causal_conv3(y[:, 0:F], halo_s, cw_ref, cb_ref, pl.program_id(1) == 0)
    act = conv * (1.0 / (1.0 + jnp.exp(-conv))) * y[:, F:2 * F]
    o_ref[...] = x + _dot(act.astype(BF16), wd_ref[...])


def _ffn(x2, g, wu, cw, cb, wd, B, T, tm):
    N, D = x2.shape
    F = cw.shape[1]
    nt = T // tm
    row = lambda b, t: (b * nt + t, 0)
    return pl.pallas_call(
        _ffn_kernel,
        grid=(B, nt),
        in_specs=[pl.BlockSpec((tm, D), row), _resident((1, D)),
                  pl.BlockSpec(wu.shape, lambda b, t: (0, 0), pipeline_mode=pl.Buffered(1)),
                  _resident(cw.shape), _resident(cb.shape),
                  pl.BlockSpec(wd.shape, lambda b, t: (0, 0), pipeline_mode=pl.Buffered(1))],
        out_specs=pl.BlockSpec((tm, D), row),
        out_shape=jax.ShapeDtypeStruct((N, D), F32),
        scratch_shapes=[pltpu.VMEM((tm + CONV_HALO, F), F32)],
        compiler_params=_cparams(2),
        name="conv_glu_ffn",
    )(x2, g, wu, cw, cb, wd)


def _s5_operators(lam_re, lam_im, log_step, b_re, b_im, c_re, c_im):
    G, P = lam_re.shape
    C = b_re.shape[-1]
    S = S5_GROUPS_PER_SLAB
    n_slab = G // S
    delta = jnp.exp(log_step)[:, None]
    mag = jnp.exp(lam_re * delta)
    bar_re = mag * jnp.cos(lam_im * delta)
    bar_im = mag * jnp.sin(lam_im * delta)
    den = lam_re * lam_re + lam_im * lam_im
    coef_re = ((bar_re - 1.0) * lam_re + bar_im * lam_im) / den
    coef_im = (bar_im * lam_re - (bar_re - 1.0) * lam_im) / den
    bb_re = coef_re[..., None] * b_re - coef_im[..., None] * b_im
    bb_im = coef_re[..., None] * b_im + coef_im[..., None] * b_re
    eye = jnp.eye(S, dtype=F32)

    def in_blocks(m):
        return jnp.einsum('kgpc,gh->kgchp', m.reshape(n_slab, S, P, C), eye).reshape(n_slab, S * C, S * P)

    def out_blocks(m):
        return jnp.einsum('kgcp,gh->kgphc', m.reshape(n_slab, S, C, P), eye).reshape(n_slab, S * P, S * C)

    bm = jnp.concatenate([in_blocks(bb_re), in_blocks(bb_im)], axis=2)
    cm = jnp.concatenate([out_blocks(c_re), -out_blocks(c_im)], axis=1)
    a_re = bar_re.reshape(1, G * P)
    a_im = bar_im.reshape(1, G * P)
    return bm.astype(BF16), cm.astype(BF16), a_re, a_im


def _even_w_in_layout(w):
    n_front = A_WIDTH + 2 * A_HEAD_DIM + IDX_HEADS * IDX_DIM + IDX_DIM + IDX_HEADS
    pad = jnp.zeros((w.shape[0], 128 - IDX_DIM - IDX_HEADS), w.dtype)
    return jnp.concatenate([w[:, :n_front], pad, w[:, n_front:]], axis=1).astype(BF16)


def kernel(x, mem, norm_mix, norm_x, norm_mem, norm_ffn, even_w_in, even_w_out, a_q_norm, a_k_norm,
           s5_lam_re, s5_lam_im, s5_log_step, s5_b_re, s5_b_im, s5_c_re, s5_c_im, s5_d, s5_w_glu, s5_b_glu,
           odd_w_in, odd_conv_w, odd_conv_b, odd_w_out, x_w_q, x_w_kv, x_w_o, x_q_norm, x_k_norm,
           f_w_up, f_conv_w, f_conv_b, f_w_down):
    B, T, D = x.shape
    M = mem.shape[1]
    depth = norm_mix.shape[0]
    assert D == 1024 and T % KEY_SUPER == 0 and even_w_in.shape[2] == 1476
    tm_proj, tm_fused, s5_steps = 512, 256, 64

    x2 = x.reshape(B * T, D)
    mem2 = mem.reshape(B * M, D)

    k_all, v_all = _mem_kv(mem2, norm_mem.reshape(depth, 1, D), x_w_kv.astype(BF16),
                           x_k_norm.reshape(depth, 1, X_HEAD_DIM))
    k_all = k_all.reshape(depth * B, M, X_WIDTH)
    v_all = v_all.reshape(depth * B, M, X_WIDTH)

    head_mean = jnp.kron(jnp.eye(A_HEADS, dtype=F32),
                         jnp.full((A_HEAD_DIM, A_HEAD_DIM), 1.0 / A_HEAD_DIM, F32)).astype(BF16)
    upper_tri = jnp.triu(jnp.ones((KEY_TILE, KEY_TILE), F32)).astype(BF16)

    for i in range(depth):
        j = i // 2
        g_mix = norm_mix[i].reshape(1, D)
        if i % 2 == 0:
            q, kv, iq, ikw, u_t = _even_in(x2, g_mix, _even_w_in_layout(even_w_in[j]), B, T, tm_proj)
            qg = jnp.tile(a_q_norm[j], A_HEADS).reshape(1, A_WIDTH)
            kg = jnp.concatenate([a_k_norm[j], jnp.zeros((128 - A_HEAD_DIM,), F32)]).reshape(1, 128)
            ya = _dsa(q, iq, ikw, kv, qg, kg, head_mean, upper_tri, B, T)
            bm, cm, a_re, a_im = _s5_operators(s5_lam_re[j], s5_lam_im[j], s5_log_step[j],
                                               s5_b_re[j], s5_b_im[j], s5_c_re[j], s5_c_im[j])
            yb_tm = _s5(u_t.reshape(T * B, 512), bm, cm, a_re, a_im, s5_d[j].reshape(1, 512),
                        s5_w_glu[j].astype(BF16), s5_b_glu[j].reshape(1, 512), B, T, s5_steps)
            w_out = even_w_out[j].astype(BF16)
            x2 = _even_out(x2, ya, yb_tm.reshape(T, B * 512), w_out[:A_WIDTH], w_out[A_WIDTH:],
                           B, T, tm_proj)
        else:
            x2 = _odd_mixer(x2, g_mix, odd_w_in[j].astype(BF16), odd_conv_w[j],
                            odd_conv_b[j].reshape(1, D), odd_w_out[j].astype(BF16), B, T, tm_fused)
        x2 = _xattn(x2, norm_x[i].reshape(1, D), x_w_q[i].astype(BF16), x_q_norm[i].reshape(1, X_HEAD_DIM),
                    k_all, v_all, x_w_o[i].astype(BF16), i, B, T, tm_fused, M)
        x2 = _ffn(x2, norm_ffn[i].reshape(1, D), f_w_up[i].astype(BF16), f_conv_w[i],
                  f_conv_b[i].reshape(1, -1), f_w_down[i].astype(BF16), B, T, tm_fused)
    return x2.reshape(B, T, D)
```

```python
import functools
import math

import jax
import jax.numpy as jnp
from jax import lax
from jax.experimental import pallas as pl
from jax.experimental.pallas import tpu as pltpu

F32 = jnp.float32
BF16 = jnp.bfloat16
I32 = jnp.int32

EPS = 1e-6
CHUNK = 64
Q_BLOCK = 128
KEY_TILE = 128
TILES_PER_SUPER = 4
KEY_SUPER = KEY_TILE * TILES_PER_SUPER
BF16_SUBLANES = 16
V_ROWS = 80
LOG2_E = 1.4426950408889634
SHIFT_SLACK = 1.02
MAX_SAFE_SHIFT = 50.0
A_HEADS = 8
A_HEAD_DIM = 64
A_WIDTH = A_HEADS * A_HEAD_DIM
IDX_HEADS = 4
IDX_DIM = 64
TOPK_MAX = 256
S5_GROUPS_PER_SLAB = 8
X_HEADS = 4
X_HEAD_DIM = 128
X_WIDTH = X_HEADS * X_HEAD_DIM
CONV_HALO = 8
MLP_CHUNK = 256
ODD_CHUNK = 256

V7X_VMEM_LIMIT_BYTES = 56 * 1024 * 1024
NEG_INF = float("-inf")


def _cparams(n_axes):
    return pltpu.CompilerParams(
        dimension_semantics=("arbitrary",) * n_axes,
        vmem_limit_bytes=V7X_VMEM_LIMIT_BYTES)


def _rms_rows(xf, g):
    ms = jnp.mean(xf * xf, axis=-1, keepdims=True)
    return xf * lax.rsqrt(ms + EPS) * g


def _dot(a, b):
    return jnp.dot(a, b, preferred_element_type=F32)


def _dot_nt(a, b):
    return lax.dot_general(a, b, (((1,), (1,)), ((), ())), preferred_element_type=F32)


def _resident(shape):
    nd = len(shape)
    return pl.BlockSpec(shape, lambda *_: (0,) * nd)


def _even_in_kernel(x_ref, g_ref, w_ref, q_ref, kv_ref, iq_ref, ikw_ref, u_ref):
    hn = _rms_rows(x_ref[...], g_ref[...])
    y = _dot(hn.astype(BF16), w_ref[...])
    q_ref[...] = y[:, 0:512].astype(BF16)
    kv_ref[...] = y[:, 512:640].astype(BF16)
    iq_ref[...] = y[:, 640:896].astype(BF16)
    ikw_ref[...] = y[:, 896:1024]
    u_ref[...] = y[:, 1024:1536].astype(BF16)


def _even_in(x2, g, w, B, T, tm):
    N, D = x2.shape
    nt = T // tm
    row = lambda b, t: (b * nt + t, 0)
    return pl.pallas_call(
        _even_in_kernel,
        grid=(B, nt),
        in_specs=[pl.BlockSpec((tm, D), row), _resident((1, D)), _resident(w.shape)],
        out_specs=[pl.BlockSpec((tm, 512), row), pl.BlockSpec((tm, 128), row),
                   pl.BlockSpec((tm, 256), row), pl.BlockSpec((tm, 128), row),
                   pl.BlockSpec((tm, 512), lambda b, t: (t, b))],
        out_shape=[jax.ShapeDtypeStruct((N, 512), BF16), jax.ShapeDtypeStruct((N, 128), BF16),
                   jax.ShapeDtypeStruct((N, 256), BF16), jax.ShapeDtypeStruct((N, 128), F32),
                   jax.ShapeDtypeStruct((T, B * 512), BF16)],
        compiler_params=_cparams(2),
        name="even_in",
    )(x2, g, w)


def _dsa_kernel(q_ref, iq_ref, iwq_ref, kv_ref, ikf_ref, qg_ref, kg_ref, bd_ref, ut_ref, o_ref,
                k1_s, ik2_s, vt_s, qs_s, iqs_s, wt_s, score_s, bias_s, m_s, acc_s, *, topk, slabs_t):
    T = kv_ref.shape[0]
    qb = pl.program_id(1)
    nst = (qb + TILES_PER_SUPER) >> (TILES_PER_SUPER.bit_length() - 1)

    lane = lax.broadcasted_iota(I32, (KEY_TILE, Q_BLOCK), 1)
    row = lax.broadcasted_iota(I32, (KEY_TILE, Q_BLOCK), 0)
    lo_half = lane < A_HEAD_DIM
    shift_lane = lane == A_HEAD_DIM

    q_scale = (A_HEAD_DIM ** -0.5) * LOG2_E
    shift = (A_HEAD_DIM * q_scale * SHIFT_SLACK) * jnp.max(jnp.abs(qg_ref[...])) * jnp.max(jnp.abs(kg_ref[...]))
    shift_is_safe = shift <= MAX_SAFE_SHIFT

    @pl.when(qb == 0)
    def _prepare_batch():
        ones_rows = jnp.where(lax.broadcasted_iota(I32, (V_ROWS - A_HEAD_DIM, KEY_TILE), 0) == 0, 1.0, 0.0)

        def key_tile(st, _):
            for t in range(TILES_PER_SUPER):
                off = pl.multiple_of(st * KEY_SUPER + t * KEY_TILE, KEY_TILE)
                kv = kv_ref[pl.ds(off, KEY_TILE), :].astype(F32)
                k = jnp.where(lo_half, kv, 0.0)
                ms = jnp.sum(k * k, axis=-1, keepdims=True) * (1.0 / A_HEAD_DIM)
                kn = k * lax.rsqrt(ms + EPS) * kg_ref[...]
                k1_s[pl.ds(off, KEY_TILE), :] = jnp.where(shift_lane, 1.0, kn).astype(BF16)
                ik = jnp.where(lo_half, ikf_ref[pl.ds(off, KEY_TILE), :], 0.0)
                ik2_s[pl.ds(off, KEY_TILE), :] = (ik + pltpu.roll(ik, 64, 1)).astype(BF16)
                v = pltpu.roll(jnp.where(lo_half, 0.0, kv), 64, 1)
                vt1 = jnp.concatenate([v.T[0:A_HEAD_DIM, :], ones_rows], axis=0)
                vt_s[st, :, t * KEY_TILE:(t + 1) * KEY_TILE] = vt1.astype(BF16)
            return 0

        lax.fori_loop(0, T // KEY_SUPER, key_tile, 0)

    q = q_ref[...].astype(F32)
    ms = _dot((q * q).astype(BF16), bd_ref[...])
    qn = q * lax.rsqrt(ms + EPS) * qg_ref[...] * q_scale
    neg_shift_lane = jnp.where(shift_lane, -1.0, 0.0) * shift
    score_dot = _dot if slabs_t else _dot_nt

    def put_slab(ref, h, slab):
        if slabs_t:
            ref[:, h * Q_BLOCK:(h + 1) * Q_BLOCK] = slab.T.astype(BF16)
        else:
            ref[h * Q_BLOCK:(h + 1) * Q_BLOCK, :] = slab.astype(BF16)

    for h in range(A_HEADS):
        blk = qn[:, 128 * (h // 2):128 * (h // 2) + 128]
        if h % 2 == 1:
            blk = pltpu.roll(blk, 64, 1)
        put_slab(qs_s, h, jnp.where(lo_half, blk, 0.0) + neg_shift_lane)
    iq = iq_ref[...].astype(F32)
    for h in range(IDX_HEADS):
        blk = iq[:, 128 * (h // 2):128 * (h // 2) + 128]
        keep = lo_half if h % 2 == 0 else jnp.logical_not(lo_half)
        put_slab(iqs_s, h, jnp.where(keep, blk, 0.0))
    wt_s[...] = iwq_ref[...].T * ((IDX_DIM ** -0.5) * (IDX_HEADS ** -0.5))

    chunk_shift = CHUNK.bit_length() - 1
    limit = (((qb * Q_BLOCK + lane) >> chunk_shift) + 1) << chunk_shift

    def score_tile(st, _):
        off = pl.multiple_of(st * KEY_SUPER, KEY_SUPER)
        x = score_dot(ik2_s[pl.ds(off, KEY_SUPER), :], iqs_s[...])
        for t in range(TILES_PER_SUPER):
            rows = slice(t * KEY_TILE, (t + 1) * KEY_TILE)
            sc = wt_s[64:65, :] * jnp.maximum(x[rows, 0:128], 0.0)
            for h in range(1, IDX_HEADS):
                sc = sc + wt_s[64 + h:65 + h, :] * jnp.maximum(x[rows, 128 * h:128 * h + 128], 0.0)
            sc = jnp.where(off + t * KEY_TILE + row < limit, sc, NEG_INF)
            score_s[pl.ds(off + t * KEY_TILE, KEY_TILE), :] = sc.astype(BF16)
        return 0

    lax.fori_loop(0, nst, score_tile, 0)

    def count_ge(cand):
        def body(st, acc):
            off = pl.multiple_of(st * KEY_SUPER, KEY_SUPER)
            tile = score_s[pl.ds(off, KEY_SUPER), :]
            parts = []
            for t in range(KEY_SUPER // BF16_SUBLANES):
                sc = tile[t * BF16_SUBLANES:(t + 1) * BF16_SUBLANES, :]
                parts.append(jnp.where(sc >= cand, jnp.ones_like(sc), jnp.zeros_like(sc)))
            while len(parts) > 1:
                parts = [parts[i] + parts[i + 1] for i in range(0, len(parts), 2)]
            return acc + parts[0]
        acc = lax.fori_loop(0, nst, body, jnp.zeros((BF16_SUBLANES, Q_BLOCK), BF16))
        return jnp.sum(acc.astype(F32), axis=0, keepdims=True)

    def pattern_to_score(u):
        bits = jnp.where(u >= 0x8000, u ^ 0x8000, u ^ 0xFFFF)
        return jnp.where(u < 0x007F, NEG_INF, pltpu.bitcast(bits << 16, F32))

    def bit_step(i, carry):
        prefix, cnt_ge = carry
        cand_u = prefix | lax.shift_left(jnp.int32(1), 15 - i)
        cand = jnp.broadcast_to(pattern_to_score(cand_u), (BF16_SUBLANES, Q_BLOCK)).astype(BF16)
        cnt = count_ge(cand)
        ok = cnt >= topk
        return jnp.where(ok, cand_u, prefix), jnp.where(ok, cnt, cnt_ge)

    prefix, cnt_ge = lax.fori_loop(
        0, 16, bit_step,
        (jnp.zeros((1, Q_BLOCK), I32), jnp.zeros((1, Q_BLOCK), F32) + (nst * KEY_SUPER).astype(F32)))
    thresh = pattern_to_score(prefix)
    excess = cnt_ge - topk

    def mask_tile(i, later_ties):
        off = pl.multiple_of((nst - 1 - i) * KEY_SUPER, KEY_SUPER)
        for t in reversed(range(TILES_PER_SUPER)):
            o = off + t * KEY_TILE
            sc = score_s[pl.ds(o, KEY_TILE), :].astype(F32)
            eq = sc == thresh
            eq_f = jnp.where(eq, 1.0, 0.0)
            suffix = later_ties + _dot(ut_ref[...], eq_f.astype(BF16))
            tie_bias = jnp.where(suffix > excess, 0.0, NEG_INF)
            bias = jnp.where(sc > thresh, 0.0, jnp.where(eq, tie_bias, NEG_INF))
            bias_s[pl.ds(o, KEY_TILE), :] = jnp.where(o + row < limit, bias, NEG_INF)
            later_ties = later_ties + jnp.sum(eq_f, axis=0, keepdims=True)
        return later_ties

    lax.fori_loop(0, nst, mask_tile, jnp.zeros((1, Q_BLOCK), F32))

    acc_s[...] = jnp.zeros(acc_s.shape, F32)

    @pl.when(shift_is_safe)
    def _attend_shifted():
        def attend_tile(st, _):
            off = pl.multiple_of(st * KEY_SUPER, KEY_SUPER)
            sp = score_dot(k1_s[pl.ds(off, KEY_SUPER), :], qs_s[...])
            bias = bias_s[pl.ds(off, KEY_SUPER), :]
            p = jnp.concatenate(
                [jnp.exp2(sp[:, h * Q_BLOCK:(h + 1) * Q_BLOCK] + bias).astype(BF16) for h in range(A_HEADS)],
                axis=1)
            acc_s[...] += _dot(vt_s[st], p)
            return 0

        lax.fori_loop(0, nst, attend_tile, 0)

    @pl.when(jnp.logical_not(shift_is_safe))
    def _attend_online():
        m_s[...] = jnp.full(m_s.shape, NEG_INF, F32)

        def attend_tile(st, _):
            off = pl.multiple_of(st * KEY_SUPER, KEY_SUPER)
            k_tile = k1_s[pl.ds(off, KEY_SUPER), :]
            bias = bias_s[pl.ds(off, KEY_SUPER), :]
            for h in range(A_HEADS):
                cols = slice(h * Q_BLOCK, (h + 1) * Q_BLOCK)
                s = score_dot(k_tile, qs_s[:, cols] if slabs_t else qs_s[cols, :]) + bias
                m_old = m_s[h:h + 1, :]
                m_new = jnp.maximum(m_old, jnp.max(s, axis=0, keepdims=True))
                m_safe = jnp.where(m_new == NEG_INF, 0.0, m_new)
                p = jnp.exp2(s - m_safe).astype(BF16)
                acc_s[:, cols] = jnp.exp2(m_old - m_safe) * acc_s[:, cols] + _dot(vt_s[st], p)
                m_s[h:h + 1, :] = m_new
            return 0

        lax.fori_loop(0, nst, attend_tile, 0)

    out_t = jnp.concatenate(
        [acc_s[0:A_HEAD_DIM, h * Q_BLOCK:(h + 1) * Q_BLOCK] / acc_s[A_HEAD_DIM:A_HEAD_DIM + 1, h * Q_BLOCK:(h + 1) * Q_BLOCK]
         for h in range(A_HEADS)], axis=0)
    o_ref[...] = out_t.T.astype(BF16)


def _dsa(q, iq, ikw, kv, qg, kg, bd, ut, B, T, slabs_t):
    N = q.shape[0]
    nqb = T // Q_BLOCK
    topk = min(TOPK_MAX, T // 4)
    qrow = lambda b, j: (b * nqb + j, 0)
    brow = lambda b, j: (b, 0)
    return pl.pallas_call(
        functools.partial(_dsa_kernel, topk=topk, slabs_t=slabs_t),
        grid=(B, nqb),
        in_specs=[pl.BlockSpec((Q_BLOCK, 512), qrow), pl.BlockSpec((Q_BLOCK, 256), qrow),
                  pl.BlockSpec((Q_BLOCK, 128), qrow), pl.BlockSpec((T, 128), brow),
                  pl.BlockSpec((T, 128), brow), _resident((1, 512)), _resident((1, 128)),
                  _resident((512, 512)), _resident((KEY_TILE, KEY_TILE))],
        out_specs=pl.BlockSpec((Q_BLOCK, 512), qrow),
        out_shape=jax.ShapeDtypeStruct((N, 512), BF16),
        scratch_shapes=[
            pltpu.VMEM((T, 128), BF16),
            pltpu.VMEM((T, 128), BF16),
            pltpu.VMEM((T // KEY_SUPER, V_ROWS, KEY_SUPER), BF16),
            pltpu.VMEM((128, A_HEADS * Q_BLOCK) if slabs_t else (A_HEADS * Q_BLOCK, 128), BF16),
            pltpu.VMEM((128, IDX_HEADS * Q_BLOCK) if slabs_t else (IDX_HEADS * Q_BLOCK, 128), BF16),
            pltpu.VMEM((128, Q_BLOCK), F32),
            pltpu.VMEM((T, Q_BLOCK), BF16),
            pltpu.VMEM((T, Q_BLOCK), F32),
            pltpu.VMEM((A_HEADS, Q_BLOCK), F32),
            pltpu.VMEM((V_ROWS, A_HEADS * Q_BLOCK), F32),
        ],
        compiler_params=_cparams(2),
        name="dsa_attention",
    )(q, iq, ikw, kv, ikw, qg, kg, bd, ut)


def _gelu_tanh(x):
    return 0.5 * x * (1.0 + jnp.tanh(math.sqrt(2.0 / math.pi) * (x + 0.044715 * (x * x * x))))


def _s5_kernel(u_ref, bm_ref, cm_ref, are_ref, aim_ref, d_ref, wg_ref, bg_ref, o_ref, buf_s, h_s,
               *, B, steps):
    n_slab = bm_ref.shape[0]
    half = bm_ref.shape[2] // 2
    width = 2 * half

    @pl.when(pl.program_id(0) == 0)
    def _init():
        h_s[...] = jnp.zeros(h_s.shape, F32)

    u = u_ref[...]
    for k in range(n_slab):
        buf_s[:, width * k:width * (k + 1)] = _dot(u[:, 128 * k:128 * (k + 1)], bm_ref[k])

    for k in range(n_slab):
        c0 = width * k
        ar = jnp.broadcast_to(are_ref[:, half * k:half * (k + 1)], (B, half))
        ai = jnp.broadcast_to(aim_ref[:, half * k:half * (k + 1)], (B, half))

        def tstep(t, carry, c0=c0, ar=ar, ai=ai):
            hr, hi = carry
            r0 = pl.multiple_of(t * B, B)
            nr = ar * hr - ai * hi + buf_s[pl.ds(r0, B), c0:c0 + half]
            ni = ar * hi + ai * hr + buf_s[pl.ds(r0, B), c0 + half:c0 + width]
            buf_s[pl.ds(r0, B), c0:c0 + half] = nr
            buf_s[pl.ds(r0, B), c0 + half:c0 + width] = ni
            return nr, ni

        hr, hi = lax.fori_loop(0, steps, tstep,
                               (h_s[:, c0:c0 + half], h_s[:, c0 + half:c0 + width]))
        h_s[:, c0:c0 + half] = hr
        h_s[:, c0 + half:c0 + width] = hi

    y = jnp.concatenate(
        [_dot(buf_s[:, width * k:width * (k + 1)].astype(BF16), cm_ref[k]) for k in range(n_slab)],
        axis=1)
    y = _gelu_tanh(y + d_ref[...] * u.astype(F32))
    z = _dot(y.astype(BF16), wg_ref[...]) + bg_ref[...]
    o_ref[...] = (y * (1.0 / (1.0 + jnp.exp(-z)))).astype(BF16)


def _s5(u_tm, bm, cm, a_re, a_im, d, wg, bg, B, T, steps):
    rows = steps * B
    n_state = bm.shape[0] * bm.shape[2]
    return pl.pallas_call(
        functools.partial(_s5_kernel, B=B, steps=steps),
        grid=(T // steps,),
        in_specs=[pl.BlockSpec((rows, 512), lambda i: (i, 0)), _resident(bm.shape), _resident(cm.shape),
                  _resident(a_re.shape), _resident(a_im.shape), _resident(d.shape),
                  _resident(wg.shape), _resident(bg.shape)],
        out_specs=pl.BlockSpec((rows, 512), lambda i: (i, 0)),
        out_shape=jax.ShapeDtypeStruct((T * B, 512), BF16),
        scratch_shapes=[pltpu.VMEM((rows, n_state), F32),
                        pltpu.VMEM((B, n_state), F32)],
        compiler_params=_cparams(1),
        name="s5_scan",
    )(u_tm, bm, cm, a_re, a_im, d, wg, bg)


def _even_out_kernel(x_ref, ya_ref, yb_ref, wa_ref, wb_ref, o_ref):
    o_ref[...] = x_ref[...] + _dot(ya_ref[...], wa_ref[...]) + _dot(yb_ref[...], wb_ref[...])


def _even_out(x2, ya, yb_t, wa, wb, B, T, tm):
    N, D = x2.shape
    nt = T // tm
    row = lambda b, t: (b * nt + t, 0)
    return pl.pallas_call(
        _even_out_kernel,
        grid=(B, nt),
        in_specs=[pl.BlockSpec((tm, D), row), pl.BlockSpec((tm, 512), row),
                  pl.BlockSpec((tm, 512), lambda b, t: (t, b)), _resident(wa.shape), _resident(wb.shape)],
        out_specs=pl.BlockSpec((tm, D), row),
        out_shape=jax.ShapeDtypeStruct((N, D), F32),
        compiler_params=_cparams(2),
        name="even_out",
    )(x2, ya, yb_t, wa, wb)


def _reset_halo(halo_s, first_tile):
    @pl.when(first_tile)
    def _zero_halo():
        halo_s[0:CONV_HALO, :] = jnp.zeros((CONV_HALO, halo_s.shape[1]), F32)


def _causal_conv3(c, halo_s, w_ref, b_ref, cols):
    tm = c.shape[0]
    halo_s[CONV_HALO:CONV_HALO + tm, cols] = c
    c1 = halo_s[CONV_HALO - 1:CONV_HALO - 1 + tm, cols]
    c2 = halo_s[CONV_HALO - 2:CONV_HALO - 2 + tm, cols]
    y = w_ref[0:1, cols] * c2 + w_ref[1:2, cols] * c1 + w_ref[2:3, cols] * c + b_ref[:, cols]
    halo_s[0:CONV_HALO, cols] = halo_s[tm:tm + CONV_HALO, cols]
    return y


def _odd_kernel(x_ref, g_ref, wi_ref, cw_ref, cb_ref, wo_ref, o_ref, halo_s):
    D = x_ref.shape[1]
    x = x_ref[...]
    hn = _rms_rows(x, g_ref[...]).astype(BF16)
    _reset_halo(halo_s, pl.program_id(1) == 0)
    acc = x
    starts = list(range(0, D, ODD_CHUNK))
    y_next = _dot(hn, wi_ref[:, 0:3 * ODD_CHUNK])
    for n, c0 in enumerate(starts):
        y = y_next
        if n + 1 < len(starts):
            c1 = starts[n + 1]
            y_next = _dot(hn, wi_ref[:, 3 * c1:3 * c1 + 3 * ODD_CHUNK])
        cols = slice(c0, c0 + ODD_CHUNK)
        conv = _causal_conv3(y[:, ODD_CHUNK:2 * ODD_CHUNK] * y[:, 2 * ODD_CHUNK:3 * ODD_CHUNK],
                             halo_s, cw_ref, cb_ref, cols)
        acc = acc + _dot((y[:, 0:ODD_CHUNK] * conv).astype(BF16), wo_ref[cols, :])
    o_ref[...] = acc


def _odd_mixer(x2, g, wi, cw, cb, wo, B, T, tm):
    N, D = x2.shape
    nt = T // tm
    row = lambda b, t: (b * nt + t, 0)
    return pl.pallas_call(
        _odd_kernel,
        grid=(B, nt),
        in_specs=[pl.BlockSpec((tm, D), row), _resident((1, D)), _resident(wi.shape),
                  _resident(cw.shape), _resident(cb.shape), _resident(wo.shape)],
        out_specs=pl.BlockSpec((tm, D), row),
        out_shape=jax.ShapeDtypeStruct((N, D), F32),
        scratch_shapes=[pltpu.VMEM((tm + CONV_HALO, D), F32)],
        compiler_params=_cparams(2),
        name="odd_mixer",
    )(x2, g, wi, cw, cb, wo)


def _mem_kv_kernel(m_ref, g_ref, w_ref, kg_ref, k_ref, v_ref):
    hn = _rms_rows(m_ref[...], g_ref[0])
    y = _dot(hn.astype(BF16), w_ref[0])
    ks = []
    for h in range(X_HEADS):
        ks.append(_rms_rows(y[:, X_HEAD_DIM * h:X_HEAD_DIM * (h + 1)], kg_ref[0]))
    k_ref[0] = jnp.concatenate(ks, axis=1).astype(BF16)
    v_ref[0] = y[:, X_WIDTH:2 * X_WIDTH].astype(BF16)


def _mem_kv(mem2, g, w, kg):
    depth = w.shape[0]
    NM, D = mem2.shape
    return pl.pallas_call(
        _mem_kv_kernel,
        grid=(depth,),
        in_specs=[_resident((NM, D)), pl.BlockSpec((1, 1, D), lambda i: (i, 0, 0)),
                  pl.BlockSpec((1, D, 2 * X_WIDTH), lambda i: (i, 0, 0)),
                  pl.BlockSpec((1, 1, X_HEAD_DIM), lambda i: (i, 0, 0))],
        out_specs=[pl.BlockSpec((1, NM, X_WIDTH), lambda i: (i, 0, 0)),
                   pl.BlockSpec((1, NM, X_WIDTH), lambda i: (i, 0, 0))],
        out_shape=[jax.ShapeDtypeStruct((depth, NM, X_WIDTH), BF16)] * 2,
        compiler_params=_cparams(1),
        name="mem_kv",
    )(mem2, g, w, kg)


def _xattn_kernel(x_ref, g_ref, wq_ref, qg_ref, k_ref, v_ref, wo_ref, o_ref):
    x = x_ref[...]
    hn = _rms_rows(x, g_ref[...])
    q = _dot(hn.astype(BF16), wq_ref[...])
    k = k_ref[0]
    v = v_ref[0]
    outs = []
    for h in range(X_HEADS):
        sl = slice(X_HEAD_DIM * h, X_HEAD_DIM * (h + 1))
        qn = _rms_rows(q[:, sl], qg_ref[...]) * (X_HEAD_DIM ** -0.5)
        s = _dot_nt(qn.astype(BF16), k[:, sl])
        p = jnp.exp(s - jnp.max(s, axis=-1, keepdims=True))
        o = _dot(p.astype(BF16), v[:, sl])
        outs.append(o / jnp.sum(p, axis=-1, keepdims=True))
    o_ref[...] = x + _dot(jnp.concatenate(outs, axis=1).astype(BF16), wo_ref[...])


def _xattn(x2, g, wq, qg, k_all, v_all, wo, layer, B, T, tm, M):
    N, D = x2.shape
    nt = T // tm
    row = lambda b, t: (b * nt + t, 0)
    mrow = lambda b, t: (layer * B + b, 0, 0)
    return pl.pallas_call(
        _xattn_kernel,
        grid=(B, nt),
        in_specs=[pl.BlockSpec((tm, D), row), _resident((1, D)), _resident(wq.shape),
                  _resident((1, X_HEAD_DIM)), pl.BlockSpec((1, M, X_WIDTH), mrow),
                  pl.BlockSpec((1, M, X_WIDTH), mrow), _resident(wo.shape)],
        out_specs=pl.BlockSpec((tm, D), row),
        out_shape=jax.ShapeDtypeStruct((N, D), F32),
        compiler_params=_cparams(2),
        name="mem_xattn",
    )(x2, g, wq, qg, k_all, v_all, wo)


def _ffn_kernel(x_ref, g_ref, wu_ref, cw_ref, cb_ref, wd_ref, o_ref, halo_s):
    F = cw_ref.shape[1]
    x = x_ref[...]
    hn = _rms_rows(x, g_ref[...]).astype(BF16)
    _reset_halo(halo_s, pl.program_id(1) == 0)
    acc = x
    starts = list(range(0, F, MLP_CHUNK))
    y_next = _dot(hn, wu_ref[:, 0:2 * MLP_CHUNK])
    for n, c0 in enumerate(starts):
        y = y_next
        if n + 1 < len(starts):
            c1 = starts[n + 1]
            y_next = _dot(hn, wu_ref[:, 2 * c1:2 * c1 + 2 * MLP_CHUNK])
        cols = slice(c0, c0 + MLP_CHUNK)
        conv = _causal_conv3(y[:, 0:MLP_CHUNK], halo_s, cw_ref, cb_ref, cols)
        act = conv * (1.0 / (1.0 + jnp.exp(-conv))) * y[:, MLP_CHUNK:2 * MLP_CHUNK]
        acc = acc + _dot(act.astype(BF16), wd_ref[cols, :])
    o_ref[...] = acc


def _ffn(x2, g, wu, cw, cb, wd, B, T, tm):
    N, D = x2.shape
    F = cw.shape[1]
    nt = T // tm
    row = lambda b, t: (b * nt + t, 0)
    return pl.pallas_call(
        _ffn_kernel,
        grid=(B, nt),
        in_specs=[pl.BlockSpec((tm, D), row), _resident((1, D)),
                  pl.BlockSpec(wu.shape, lambda b, t: (0, 0), pipeline_mode=pl.Buffered(1)),
                  _resident(cw.shape), _resident(cb.shape),
                  pl.BlockSpec(wd.shape, lambda b, t: (0, 0), pipeline_mode=pl.Buffered(1))],
        out_specs=pl.BlockSpec((tm, D), row),
        out_shape=jax.ShapeDtypeStruct((N, D), F32),
        scratch_shapes=[pltpu.VMEM((tm + CONV_HALO, F), F32)],
        compiler_params=_cparams(2),
        name="conv_glu_ffn",
    )(x2, g, wu, cw, cb, wd)


def _s5_operators(lam_re, lam_im, log_step, b_re, b_im, c_re, c_im):
    G, P = lam_re.shape
    C = b_re.shape[-1]
    S = S5_GROUPS_PER_SLAB
    n_slab = G // S
    delta = jnp.exp(log_step)[:, None]
    mag = jnp.exp(lam_re * delta)
    bar_re = mag * jnp.cos(lam_im * delta)
    bar_im = mag * jnp.sin(lam_im * delta)
    den = lam_re * lam_re + lam_im * lam_im
    coef_re = ((bar_re - 1.0) * lam_re + bar_im * lam_im) / den
    coef_im = (bar_im * lam_re - (bar_re - 1.0) * lam_im) / den
    bb_re = coef_re[..., None] * b_re - coef_im[..., None] * b_im
    bb_im = coef_re[..., None] * b_im + coef_im[..., None] * b_re
    eye = jnp.eye(S, dtype=F32)

    def in_blocks(m):
        return jnp.einsum('kgpc,gh->kgchp', m.reshape(n_slab, S, P, C), eye).reshape(n_slab, S * C, S * P)

    def out_blocks(m):
        return jnp.einsum('kgcp,gh->kgphc', m.reshape(n_slab, S, C, P), eye).reshape(n_slab, S * P, S * C)

    bm = jnp.concatenate([in_blocks(bb_re), in_blocks(bb_im)], axis=2)
    cm = jnp.concatenate([out_blocks(c_re), -out_blocks(c_im)], axis=1)
    a_re = bar_re.reshape(1, G * P)
    a_im = bar_im.reshape(1, G * P)
    return bm.astype(BF16), cm.astype(BF16), a_re, a_im


def _chunk_interleave(w, n_parts, chunk):
    k, n = w.shape
    width = n // n_parts
    return w.reshape(k, n_parts, width // chunk, chunk).transpose(0, 2, 1, 3).reshape(k, n)


def _even_w_in_layout(w):
    n_front = A_WIDTH + 2 * A_HEAD_DIM + IDX_HEADS * IDX_DIM + IDX_DIM + IDX_HEADS
    pad = jnp.zeros((w.shape[0], 128 - IDX_DIM - IDX_HEADS), w.dtype)
    return jnp.concatenate([w[:, :n_front], pad, w[:, n_front:]], axis=1).astype(BF16)


def kernel(x, mem, norm_mix, norm_x, norm_mem, norm_ffn, even_w_in, even_w_out, a_q_norm, a_k_norm,
           s5_lam_re, s5_lam_im, s5_log_step, s5_b_re, s5_b_im, s5_c_re, s5_c_im, s5_d, s5_w_glu, s5_b_glu,
           odd_w_in, odd_conv_w, odd_conv_b, odd_w_out, x_w_q, x_w_kv, x_w_o, x_q_norm, x_k_norm,
           f_w_up, f_conv_w, f_conv_b, f_w_down):
    B, T, D = x.shape
    M = mem.shape[1]
    depth = norm_mix.shape[0]
    assert D == 1024 and T % KEY_SUPER == 0 and even_w_in.shape[2] == 1476
    tm_proj, tm_fused, s5_steps = 512, 256, 64

    x2 = x.reshape(B * T, D)
    mem2 = mem.reshape(B * M, D)

    k_all, v_all = _mem_kv(mem2, norm_mem.reshape(depth, 1, D), x_w_kv.astype(BF16),
                           x_k_norm.reshape(depth, 1, X_HEAD_DIM))
    k_all = k_all.reshape(depth * B, M, X_WIDTH)
    v_all = v_all.reshape(depth * B, M, X_WIDTH)

    head_mean = jnp.kron(jnp.eye(A_HEADS, dtype=F32),
                         jnp.full((A_HEAD_DIM, A_HEAD_DIM), 1.0 / A_HEAD_DIM, F32)).astype(BF16)
    upper_tri = jnp.triu(jnp.ones((KEY_TILE, KEY_TILE), F32)).astype(BF16)

    for i in range(depth):
        j = i // 2
        g_mix = norm_mix[i].reshape(1, D)
        if i % 2 == 0:
            q, kv, iq, ikw, u_t = _even_in(x2, g_mix, _even_w_in_layout(even_w_in[j]), B, T, tm_proj)
            qg = jnp.tile(a_q_norm[j], A_HEADS).reshape(1, A_WIDTH)
            kg = jnp.concatenate([a_k_norm[j], jnp.zeros((128 - A_HEAD_DIM,), F32)]).reshape(1, 128)
            ya = _dsa(q, iq, ikw, kv, qg, kg, head_mean, upper_tri, B, T, slabs_t=(j % 2 == 1))
            bm, cm, a_re, a_im = _s5_operators(s5_lam_re[j], s5_lam_im[j], s5_log_step[j],
                                               s5_b_re[j], s5_b_im[j], s5_c_re[j], s5_c_im[j])
            yb_tm = _s5(u_t.reshape(T * B, 512), bm, cm, a_re, a_im, s5_d[j].reshape(1, 512),
                        s5_w_glu[j].astype(BF16), s5_b_glu[j].reshape(1, 512), B, T, s5_steps)
            w_out = even_w_out[j].astype(BF16)
            x2 = _even_out(x2, ya, yb_tm.reshape(T, B * 512), w_out[:A_WIDTH], w_out[A_WIDTH:],
                           B, T, tm_proj)
        else:
            x2 = _odd_mixer(x2, g_mix, _chunk_interleave(odd_w_in[j].astype(BF16), 3, ODD_CHUNK), odd_conv_w[j],
                            odd_conv_b[j].reshape(1, D), odd_w_out[j].astype(BF16), B, T, tm_fused)
        x2 = _xattn(x2, norm_x[i].reshape(1, D), x_w_q[i].astype(BF16), x_q_norm[i].reshape(1, X_HEAD_DIM),
                    k_all, v_all, x_w_o[i].astype(BF16), i, B, T, tm_fused, M)
        x2 = _ffn(x2, norm_ffn[i].reshape(1, D), _chunk_interleave(f_w_up[i].astype(BF16), 2, MLP_CHUNK), f_conv_w[i],
                  f_conv_b[i].reshape(1, -1), f_w_down[i].astype(BF16), B, T, tm_fused)
    return x2.reshape(B, T, D)
```

```python
import functools
import math

import jax
import jax.numpy as jnp
from jax import lax
from jax.experimental import pallas as pl
from jax.experimental.pallas import tpu as pltpu

F32 = jnp.float32
BF16 = jnp.bfloat16
I32 = jnp.int32

EPS = 1e-6
CHUNK = 64
Q_BLOCK = 128
KEY_TILE = 128
TILES_PER_SUPER = 4
KEY_SUPER = KEY_TILE * TILES_PER_SUPER
KEY_BITS = 16
KEY_MIN = -(1 << (KEY_BITS - 1))
V_ROWS = 80
LOG2_E = 1.4426950408889634
SHIFT_SLACK = 1.02
MAX_SAFE_SHIFT = 50.0
A_HEADS = 8
A_HEAD_DIM = 64
A_WIDTH = A_HEADS * A_HEAD_DIM
IDX_HEADS = 4
IDX_DIM = 64
TOPK_MAX = 256
S5_GROUPS_PER_SLAB = 8
X_HEADS = 4
X_HEAD_DIM = 128
X_WIDTH = X_HEADS * X_HEAD_DIM
CONV_HALO = 8
MLP_CHUNK = 256
ODD_CHUNK = 256

V7X_VMEM_LIMIT_BYTES = 56 * 1024 * 1024
NEG_INF = float("-inf")


def _cparams(n_axes):
    return pltpu.CompilerParams(
        dimension_semantics=("arbitrary",) * n_axes,
        vmem_limit_bytes=V7X_VMEM_LIMIT_BYTES)


def _rms_rows(xf, g):
    ms = jnp.mean(xf * xf, axis=-1, keepdims=True)
    return xf * lax.rsqrt(ms + EPS) * g


def _dot(a, b):
    return jnp.dot(a, b, preferred_element_type=F32)


def _dot_nt(a, b):
    return lax.dot_general(a, b, (((1,), (1,)), ((), ())), preferred_element_type=F32)


def _resident(shape):
    nd = len(shape)
    return pl.BlockSpec(shape, lambda *_: (0,) * nd)


def _even_in_kernel(x_ref, g_ref, w_ref, q_ref, kv_ref, iq_ref, ikw_ref, u_ref):
    hn = _rms_rows(x_ref[...], g_ref[...])
    y = _dot(hn.astype(BF16), w_ref[...])
    q_ref[...] = y[:, 0:512].astype(BF16)
    kv_ref[...] = y[:, 512:640].astype(BF16)
    iq_ref[...] = y[:, 640:896].astype(BF16)
    ikw_ref[...] = y[:, 896:1024]
    u_ref[...] = y[:, 1024:1536].astype(BF16)


def _even_in(x2, g, w, B, T, tm):
    N, D = x2.shape
    nt = T // tm
    row = lambda b, t: (b * nt + t, 0)
    return pl.pallas_call(
        _even_in_kernel,
        grid=(B, nt),
        in_specs=[pl.BlockSpec((tm, D), row), _resident((1, D)), _resident(w.shape)],
        out_specs=[pl.BlockSpec((tm, 512), row), pl.BlockSpec((tm, 128), row),
                   pl.BlockSpec((tm, 256), row), pl.BlockSpec((tm, 128), row),
                   pl.BlockSpec((tm, 512), lambda b, t: (t, b))],
        out_shape=[jax.ShapeDtypeStruct((N, 512), BF16), jax.ShapeDtypeStruct((N, 128), BF16),
                   jax.ShapeDtypeStruct((N, 256), BF16), jax.ShapeDtypeStruct((N, 128), F32),
                   jax.ShapeDtypeStruct((T, B * 512), BF16)],
        compiler_params=_cparams(2),
        name="even_in",
    )(x2, g, w)


def _dsa_kernel(q_ref, iq_ref, iwq_ref, kv_ref, ikf_ref, qg_ref, kg_ref, bd_ref, ut_ref, o_ref,
                k1_s, ik2_s, vt_s, qs_s, iqs_s, wt_s, key_s, bias_s, m_s, acc_s, *, topk):
    T = kv_ref.shape[0]
    qb = pl.program_id(1)
    nst = (qb + TILES_PER_SUPER) >> (TILES_PER_SUPER.bit_length() - 1)

    lane = lax.broadcasted_iota(I32, (KEY_TILE, Q_BLOCK), 1)
    row = lax.broadcasted_iota(I32, (KEY_TILE, Q_BLOCK), 0)
    lo_half = lane < A_HEAD_DIM
    shift_lane = lane == A_HEAD_DIM

    q_scale = (A_HEAD_DIM ** -0.5) * LOG2_E
    shift = (A_HEAD_DIM * q_scale * SHIFT_SLACK) * jnp.max(jnp.abs(qg_ref[...])) * jnp.max(jnp.abs(kg_ref[...]))
    shift_is_safe = shift <= MAX_SAFE_SHIFT

    @pl.when(qb == 0)
    def _prepare_batch():
        ones_rows = jnp.where(lax.broadcasted_iota(I32, (V_ROWS - A_HEAD_DIM, KEY_TILE), 0) == 0, 1.0, 0.0)

        def key_tile(st, _):
            for t in range(TILES_PER_SUPER):
                off = pl.multiple_of(st * KEY_SUPER + t * KEY_TILE, KEY_TILE)
                kv = kv_ref[pl.ds(off, KEY_TILE), :].astype(F32)
                k = jnp.where(lo_half, kv, 0.0)
                ms = jnp.sum(k * k, axis=-1, keepdims=True) * (1.0 / A_HEAD_DIM)
                kn = k * lax.rsqrt(ms + EPS) * kg_ref[...]
                k1_s[pl.ds(off, KEY_TILE), :] = jnp.where(shift_lane, 1.0, kn).astype(BF16)
                ik = jnp.where(lo_half, ikf_ref[pl.ds(off, KEY_TILE), :], 0.0)
                ik2_s[pl.ds(off, KEY_TILE), :] = (ik + pltpu.roll(ik, 64, 1)).astype(BF16)
                v = pltpu.roll(jnp.where(lo_half, 0.0, kv), 64, 1)
                vt1 = jnp.concatenate([v.T[0:A_HEAD_DIM, :], ones_rows], axis=0)
                vt_s[st, :, t * KEY_TILE:(t + 1) * KEY_TILE] = vt1.astype(BF16)
            return 0

        lax.fori_loop(0, T // KEY_SUPER, key_tile, 0)

    q = q_ref[...].astype(F32)
    ms = _dot((q * q).astype(BF16), bd_ref[...])
    qn = q * lax.rsqrt(ms + EPS) * qg_ref[...] * q_scale
    neg_shift_lane = jnp.where(shift_lane, -1.0, 0.0) * shift
    for h in range(A_HEADS):
        blk = qn[:, 128 * (h // 2):128 * (h // 2) + 128]
        if h % 2 == 1:
            blk = pltpu.roll(blk, 64, 1)
        qs_s[h * Q_BLOCK:(h + 1) * Q_BLOCK, :] = (jnp.where(lo_half, blk, 0.0) + neg_shift_lane).astype(BF16)
    iq = iq_ref[...].astype(F32)
    for h in range(IDX_HEADS):
        blk = iq[:, 128 * (h // 2):128 * (h // 2) + 128]
        keep = lo_half if h % 2 == 0 else jnp.logical_not(lo_half)
        iqs_s[h * Q_BLOCK:(h + 1) * Q_BLOCK, :] = jnp.where(keep, blk, 0.0).astype(BF16)
    wt_s[...] = iwq_ref[...].T * ((IDX_DIM ** -0.5) * (IDX_HEADS ** -0.5))

    chunk_shift = CHUNK.bit_length() - 1
    limit = (((qb * Q_BLOCK + lane) >> chunk_shift) + 1) << chunk_shift

    def score_tile(st, _):
        off = pl.multiple_of(st * KEY_SUPER, KEY_SUPER)
        x = _dot_nt(ik2_s[pl.ds(off, KEY_SUPER), :], iqs_s[...])
        for t in range(TILES_PER_SUPER):
            rows = slice(t * KEY_TILE, (t + 1) * KEY_TILE)
            sc = wt_s[64:65, :] * jnp.maximum(x[rows, 0:128], 0.0)
            for h in range(1, IDX_HEADS):
                sc = sc + wt_s[64 + h:65 + h, :] * jnp.maximum(x[rows, 128 * h:128 * h + 128], 0.0)
            sc = jnp.where(sc == 0.0, 0.0, sc)
            sc = jnp.where(off + t * KEY_TILE + row < limit, sc, NEG_INF)
            bits = pltpu.bitcast(sc.astype(BF16).astype(F32), I32) >> 16
            key_s[pl.ds(off + t * KEY_TILE, KEY_TILE), :] = bits ^ ((bits >> 31) & 0x7FFF)
        return 0

    lax.fori_loop(0, nst, score_tile, 0)

    def count_ge(cand):
        def body(st, acc):
            off = pl.multiple_of(st * KEY_SUPER, KEY_SUPER)
            parts = []
            for t in range(TILES_PER_SUPER):
                ge = jnp.where(key_s[pl.ds(off + t * KEY_TILE, KEY_TILE), :] >= cand, 1, 0)
                parts.append(jnp.sum(ge.reshape(KEY_TILE // 8, 8, Q_BLOCK), axis=0))
            return acc + ((parts[0] + parts[1]) + (parts[2] + parts[3]))
        acc = lax.fori_loop(0, nst, body, jnp.zeros((8, Q_BLOCK), I32))
        return jnp.sum(acc, axis=0, keepdims=True)

    def bit_step(i, carry):
        prefix, cnt_ge = carry
        cand_u = prefix | lax.shift_left(jnp.int32(1), KEY_BITS - 1 - i)
        cnt = count_ge(cand_u + KEY_MIN)
        ok = cnt >= topk
        return jnp.where(ok, cand_u, prefix), jnp.where(ok, cnt, cnt_ge)

    prefix, cnt_ge = lax.fori_loop(
        0, KEY_BITS, bit_step,
        (jnp.zeros((1, Q_BLOCK), I32), jnp.zeros((1, Q_BLOCK), I32) + nst * KEY_SUPER))
    thresh = prefix + KEY_MIN
    excess = (cnt_ge - topk).astype(F32)

    def mask_tile(i, later_ties):
        off = pl.multiple_of((nst - 1 - i) * KEY_SUPER, KEY_SUPER)
        for t in reversed(range(TILES_PER_SUPER)):
            o = off + t * KEY_TILE
            key = key_s[pl.ds(o, KEY_TILE), :]
            eq = key == thresh
            eq_f = jnp.where(eq, 1.0, 0.0)
            suffix = later_ties + _dot(ut_ref[...], eq_f.astype(BF16))
            tie_bias = jnp.where(suffix > excess, 0.0, NEG_INF)
            bias = jnp.where(key > thresh, 0.0, jnp.where(eq, tie_bias, NEG_INF))
            bias_s[pl.ds(o, KEY_TILE), :] = jnp.where(o + row < limit, bias, NEG_INF)
            later_ties = later_ties + jnp.sum(eq_f, axis=0, keepdims=True)
        return later_ties

    lax.fori_loop(0, nst, mask_tile, jnp.zeros((1, Q_BLOCK), F32))

    acc_s[...] = jnp.zeros(acc_s.shape, F32)

    @pl.when(shift_is_safe)
    def _attend_shifted():
        def scores(st):
            off = pl.multiple_of(st * KEY_SUPER, KEY_SUPER)
            return _dot_nt(k1_s[pl.ds(off, KEY_SUPER), :], qs_s[...])

        def attend(st, sp):
            off = pl.multiple_of(st * KEY_SUPER, KEY_SUPER)
            bias = bias_s[pl.ds(off, KEY_SUPER), :]
            p = jnp.concatenate(
                [jnp.exp2(sp[:, h * Q_BLOCK:(h + 1) * Q_BLOCK] + bias).astype(BF16) for h in range(A_HEADS)],
                axis=1)
            acc_s[...] += _dot(vt_s[st], p)

        def attend_pair(i, _):
            sp0 = scores(2 * i)
            sp1 = scores(2 * i + 1)
            attend(2 * i, sp0)
            attend(2 * i + 1, sp1)
            return 0

        lax.fori_loop(0, nst >> 1, attend_pair, 0)

        @pl.when((nst & 1) == 1)
        def _odd_tail():
            attend(nst - 1, scores(nst - 1))

    @pl.when(jnp.logical_not(shift_is_safe))
    def _attend_online():
        m_s[...] = jnp.full(m_s.shape, NEG_INF, F32)

        def attend_tile(st, _):
            off = pl.multiple_of(st * KEY_SUPER, KEY_SUPER)
            k_tile = k1_s[pl.ds(off, KEY_SUPER), :]
            bias = bias_s[pl.ds(off, KEY_SUPER), :]
            for h in range(A_HEADS):
                cols = slice(h * Q_BLOCK, (h + 1) * Q_BLOCK)
                s = _dot_nt(k_tile, qs_s[cols, :]) + bias
                m_old = m_s[h:h + 1, :]
                m_new = jnp.maximum(m_old, jnp.max(s, axis=0, keepdims=True))
                m_safe = jnp.where(m_new == NEG_INF, 0.0, m_new)
                p = jnp.exp2(s - m_safe).astype(BF16)
                acc_s[:, cols] = jnp.exp2(m_old - m_safe) * acc_s[:, cols] + _dot(vt_s[st], p)
                m_s[h:h + 1, :] = m_new
            return 0

        lax.fori_loop(0, nst, attend_tile, 0)

    out_t = jnp.concatenate(
        [acc_s[0:A_HEAD_DIM, h * Q_BLOCK:(h + 1) * Q_BLOCK] / acc_s[A_HEAD_DIM:A_HEAD_DIM + 1, h * Q_BLOCK:(h + 1) * Q_BLOCK]
         for h in range(A_HEADS)], axis=0)
    o_ref[...] = out_t.T.astype(BF16)


def _dsa(q, iq, ikw, kv, qg, kg, bd, ut, B, T):
    N = q.shape[0]
    nqb = T // Q_BLOCK
    topk = min(TOPK_MAX, T // 4)
    qrow = lambda b, j: (b * nqb + j, 0)
    brow = lambda b, j: (b, 0)
    return pl.pallas_call(
        functools.partial(_dsa_kernel, topk=topk),
        grid=(B, nqb),
        in_specs=[pl.BlockSpec((Q_BLOCK, 512), qrow), pl.BlockSpec((Q_BLOCK, 256), qrow),
                  pl.BlockSpec((Q_BLOCK, 128), qrow), pl.BlockSpec((T, 128), brow),
                  pl.BlockSpec((T, 128), brow), _resident((1, 512)), _resident((1, 128)),
                  _resident((512, 512)), _resident((KEY_TILE, KEY_TILE))],
        out_specs=pl.BlockSpec((Q_BLOCK, 512), qrow),
        out_shape=jax.ShapeDtypeStruct((N, 512), BF16),
        scratch_shapes=[
            pltpu.VMEM((T, 128), BF16),
            pltpu.VMEM((T, 128), BF16),
            pltpu.VMEM((T // KEY_SUPER, V_ROWS, KEY_SUPER), BF16),
            pltpu.VMEM((A_HEADS * Q_BLOCK, 128), BF16),
            pltpu.VMEM((IDX_HEADS * Q_BLOCK, 128), BF16),
            pltpu.VMEM((128, Q_BLOCK), F32),
            pltpu.VMEM((T, Q_BLOCK), I32),
            pltpu.VMEM((T, Q_BLOCK), F32),
            pltpu.VMEM((A_HEADS, Q_BLOCK), F32),
            pltpu.VMEM((V_ROWS, A_HEADS * Q_BLOCK), F32),
        ],
        compiler_params=_cparams(2),
        name="dsa_attention",
    )(q, iq, ikw, kv, ikw, qg, kg, bd, ut)


def _gelu_tanh(x):
    return 0.5 * x * (1.0 + jnp.tanh(math.sqrt(2.0 / math.pi) * (x + 0.044715 * (x * x * x))))


def _s5_kernel(u_ref, bm_ref, cm_ref, are_ref, aim_ref, d_ref, wg_ref, bg_ref, o_ref, buf_s, h_s,
               *, B, steps):
    n_slab = bm_ref.shape[0]
    half = bm_ref.shape[2] // 2
    width = 2 * half

    @pl.when(pl.program_id(0) == 0)
    def _init():
        h_s[...] = jnp.zeros(h_s.shape, F32)

    u = u_ref[...]
    for k in range(n_slab):
        buf_s[:, width * k:width * (k + 1)] = _dot(u[:, 128 * k:128 * (k + 1)], bm_ref[k])

    for k in range(n_slab):
        c0 = width * k
        ar = jnp.broadcast_to(are_ref[:, half * k:half * (k + 1)], (B, half))
        ai = jnp.broadcast_to(aim_ref[:, half * k:half * (k + 1)], (B, half))

        def tstep(t, carry, c0=c0, ar=ar, ai=ai):
            hr, hi = carry
            r0 = pl.multiple_of(t * B, B)
            nr = ar * hr - ai * hi + buf_s[pl.ds(r0, B), c0:c0 + half]
            ni = ar * hi + ai * hr + buf_s[pl.ds(r0, B), c0 + half:c0 + width]
            buf_s[pl.ds(r0, B), c0:c0 + half] = nr
            buf_s[pl.ds(r0, B), c0 + half:c0 + width] = ni
            return nr, ni

        hr, hi = lax.fori_loop(0, steps, tstep,
                               (h_s[:, c0:c0 + half], h_s[:, c0 + half:c0 + width]))
        h_s[:, c0:c0 + half] = hr
        h_s[:, c0 + half:c0 + width] = hi

    y = jnp.concatenate(
        [_dot(buf_s[:, width * k:width * (k + 1)].astype(BF16), cm_ref[k]) for k in range(n_slab)],
        axis=1)
    y = _gelu_tanh(y + d_ref[...] * u.astype(F32))
    z = _dot(y.astype(BF16), wg_ref[...]) + bg_ref[...]
    o_ref[...] = (y * (1.0 / (1.0 + jnp.exp(-z)))).astype(BF16)


def _s5(u_tm, bm, cm, a_re, a_im, d, wg, bg, B, T, steps):
    rows = steps * B
    n_state = bm.shape[0] * bm.shape[2]
    return pl.pallas_call(
        functools.partial(_s5_kernel, B=B, steps=steps),
        grid=(T // steps,),
        in_specs=[pl.BlockSpec((rows, 512), lambda i: (i, 0)), _resident(bm.shape), _resident(cm.shape),
                  _resident(a_re.shape), _resident(a_im.shape), _resident(d.shape),
                  _resident(wg.shape), _resident(bg.shape)],
        out_specs=pl.BlockSpec((rows, 512), lambda i: (i, 0)),
        out_shape=jax.ShapeDtypeStruct((T * B, 512), BF16),
        scratch_shapes=[pltpu.VMEM((rows, n_state), F32),
                        pltpu.VMEM((B, n_state), F32)],
        compiler_params=_cparams(1),
        name="s5_scan",
    )(u_tm, bm, cm, a_re, a_im, d, wg, bg)


def _even_out_kernel(x_ref, ya_ref, yb_ref, wa_ref, wb_ref, o_ref):
    o_ref[...] = x_ref[...] + _dot(ya_ref[...], wa_ref[...]) + _dot(yb_ref[...], wb_ref[...])


def _even_out(x2, ya, yb_t, wa, wb, B, T, tm):
    N, D = x2.shape
    nt = T // tm
    row = lambda b, t: (b * nt + t, 0)
    return pl.pallas_call(
        _even_out_kernel,
        grid=(B, nt),
        in_specs=[pl.BlockSpec((tm, D), row), pl.BlockSpec((tm, 512), row),
                  pl.BlockSpec((tm, 512), lambda b, t: (t, b)), _resident(wa.shape), _resident(wb.shape)],
        out_specs=pl.BlockSpec((tm, D), row),
        out_shape=jax.ShapeDtypeStruct((N, D), F32),
        compiler_params=_cparams(2),
        name="even_out",
    )(x2, ya, yb_t, wa, wb)


def _reset_halo(halo_s, first_tile):
    @pl.when(first_tile)
    def _zero_halo():
        halo_s[0:CONV_HALO, :] = jnp.zeros((CONV_HALO, halo_s.shape[1]), F32)


def _causal_conv3(c, halo_s, w_ref, b_ref, cols):
    tm = c.shape[0]
    halo_s[CONV_HALO:CONV_HALO + tm, cols] = c
    c1 = halo_s[CONV_HALO - 1:CONV_HALO - 1 + tm, cols]
    c2 = halo_s[CONV_HALO - 2:CONV_HALO - 2 + tm, cols]
    y = w_ref[0:1, cols] * c2 + w_ref[1:2, cols] * c1 + w_ref[2:3, cols] * c + b_ref[:, cols]
    halo_s[0:CONV_HALO, cols] = halo_s[tm:tm + CONV_HALO, cols]
    return y


def _odd_kernel(x_ref, g_ref, wi_ref, cw_ref, cb_ref, wo_ref, o_ref, halo_s):
    D = x_ref.shape[1]
    x = x_ref[...]
    hn = _rms_rows(x, g_ref[...]).astype(BF16)
    _reset_halo(halo_s, pl.program_id(1) == 0)
    acc = x
    starts = list(range(0, D, ODD_CHUNK))

    def in_proj(c0):
        return tuple(_dot(hn, wi_ref[:, part * D + c0:part * D + c0 + ODD_CHUNK]) for part in range(3))

    nxt = in_proj(0)
    for n, c0 in enumerate(starts):
        gb, gc, z = nxt
        if n + 1 < len(starts):
            nxt = in_proj(starts[n + 1])
        cols = slice(c0, c0 + ODD_CHUNK)
        conv = _causal_conv3(gc * z, halo_s, cw_ref, cb_ref, cols)
        acc = acc + _dot((gb * conv).astype(BF16), wo_ref[cols, :])
    o_ref[...] = acc


def _odd_mixer(x2, g, wi, cw, cb, wo, B, T, tm):
    N, D = x2.shape
    nt = T // tm
    row = lambda b, t: (b * nt + t, 0)
    return pl.pallas_call(
        _odd_kernel,
        grid=(B, nt),
        in_specs=[pl.BlockSpec((tm, D), row), _resident((1, D)), _resident(wi.shape),
                  _resident(cw.shape), _resident(cb.shape), _resident(wo.shape)],
        out_specs=pl.BlockSpec((tm, D), row),
        out_shape=jax.ShapeDtypeStruct((N, D), F32),
        scratch_shapes=[pltpu.VMEM((tm + CONV_HALO, D), F32)],
        compiler_params=_cparams(2),
        name="odd_mixer",
    )(x2, g, wi, cw, cb, wo)


def _mem_kv_kernel(m_ref, g_ref, w_ref, kg_ref, k_ref, v_ref):
    hn = _rms_rows(m_ref[...], g_ref[0])
    y = _dot(hn.astype(BF16), w_ref[0])
    ks = []
    for h in range(X_HEADS):
        ks.append(_rms_rows(y[:, X_HEAD_DIM * h:X_HEAD_DIM * (h + 1)], kg_ref[0]))
    k_ref[0] = jnp.concatenate(ks, axis=1).astype(BF16)
    v_ref[0] = y[:, X_WIDTH:2 * X_WIDTH].astype(BF16)


def _mem_kv(mem2, g, w, kg):
    depth = w.shape[0]
    NM, D = mem2.shape
    return pl.pallas_call(
        _mem_kv_kernel,
        grid=(depth,),
        in_specs=[_resident((NM, D)), pl.BlockSpec((1, 1, D), lambda i: (i, 0, 0)),
                  pl.BlockSpec((1, D, 2 * X_WIDTH), lambda i: (i, 0, 0)),
                  pl.BlockSpec((1, 1, X_HEAD_DIM), lambda i: (i, 0, 0))],
        out_specs=[pl.BlockSpec((1, NM, X_WIDTH), lambda i: (i, 0, 0)),
                   pl.BlockSpec((1, NM, X_WIDTH), lambda i: (i, 0, 0))],
        out_shape=[jax.ShapeDtypeStruct((depth, NM, X_WIDTH), BF16)] * 2,
        compiler_params=_cparams(1),
        name="mem_kv",
    )(mem2, g, w, kg)


def _xattn_kernel(x_ref, g_ref, wq_ref, qg_ref, k_ref, v_ref, wo_ref, o_ref):
    x = x_ref[...]
    hn = _rms_rows(x, g_ref[...])
    q = _dot(hn.astype(BF16), wq_ref[...])
    k = k_ref[0]
    v = v_ref[0]
    heads = [slice(X_HEAD_DIM * h, X_HEAD_DIM * (h + 1)) for h in range(X_HEADS)]
    scores = [_dot_nt((_rms_rows(q[:, sl], qg_ref[...]) * (X_HEAD_DIM ** -0.5)).astype(BF16), k[:, sl])
              for sl in heads]
    outs = []
    for sl, s in zip(heads, scores):
        p = jnp.exp(s - jnp.max(s, axis=-1, keepdims=True))
        outs.append(_dot(p.astype(BF16), v[:, sl]) / jnp.sum(p, axis=-1, keepdims=True))
    o_ref[...] = x + _dot(jnp.concatenate(outs, axis=1).astype(BF16), wo_ref[...])


def _xattn(x2, g, wq, qg, k_all, v_all, wo, layer, B, T, tm, M):
    N, D = x2.shape
    nt = T // tm
    row = lambda b, t: (b * nt + t, 0)
    mrow = lambda b, t: (layer * B + b, 0, 0)
    return pl.pallas_call(
        _xattn_kernel,
        grid=(B, nt),
        in_specs=[pl.BlockSpec((tm, D), row), _resident((1, D)), _resident(wq.shape),
                  _resident((1, X_HEAD_DIM)), pl.BlockSpec((1, M, X_WIDTH), mrow),
                  pl.BlockSpec((1, M, X_WIDTH), mrow), _resident(wo.shape)],
        out_specs=pl.BlockSpec((tm, D), row),
        out_shape=jax.ShapeDtypeStruct((N, D), F32),
        compiler_params=_cparams(2),
        name="mem_xattn",
    )(x2, g, wq, qg, k_all, v_all, wo)


def _ffn_kernel(x_ref, g_ref, wu_ref, cw_ref, cb_ref, wd_ref, o_ref, halo_s):
    F = cw_ref.shape[1]
    x = x_ref[...]
    hn = _rms_rows(x, g_ref[...]).astype(BF16)
    _reset_halo(halo_s, pl.program_id(1) == 0)
    acc = x
    starts = list(range(0, F, MLP_CHUNK))

    def up_proj(c0):
        return _dot(hn, wu_ref[:, c0:c0 + MLP_CHUNK]), _dot(hn, wu_ref[:, F + c0:F + c0 + MLP_CHUNK])

    nxt = up_proj(0)
    for n, c0 in enumerate(starts):
        gate, up = nxt
        if n + 1 < len(starts):
            nxt = up_proj(starts[n + 1])
        cols = slice(c0, c0 + MLP_CHUNK)
        conv = _causal_conv3(gate, halo_s, cw_ref, cb_ref, cols)
        act = conv * (1.0 / (1.0 + jnp.exp(-conv))) * up
        acc = acc + _dot(act.astype(BF16), wd_ref[cols, :])
    o_ref[...] = acc


def _ffn(x2, g, wu, cw, cb, wd, B, T, tm):
    N, D = x2.shape
    F = cw.shape[1]
    nt = T // tm
    row = lambda b, t: (b * nt + t, 0)
    return pl.pallas_call(
        _ffn_kernel,
        grid=(B, nt),
        in_specs=[pl.BlockSpec((tm, D), row), _resident((1, D)),
                  pl.BlockSpec(wu.shape, lambda b, t: (0, 0), pipeline_mode=pl.Buffered(1)),
                  _resident(cw.shape), _resident(cb.shape),
                  pl.BlockSpec(wd.shape, lambda b, t: (0, 0), pipeline_mode=pl.Buffered(1))],
        out_specs=pl.BlockSpec((tm, D), row),
        out_shape=jax.ShapeDtypeStruct((N, D), F32),
        scratch_shapes=[pltpu.VMEM((tm + CONV_HALO, F), F32)],
        compiler_params=_cparams(2),
        name="conv_glu_ffn",
    )(x2, g, wu, cw, cb, wd)


def _s5_operators(lam_re, lam_im, log_step, b_re, b_im, c_re, c_im):
    G, P = lam_re.shape
    C = b_re.shape[-1]
    S = S5_GROUPS_PER_SLAB
    n_slab = G // S
    delta = jnp.exp(log_step)[:, None]
    mag = jnp.exp(lam_re * delta)
    bar_re = mag * jnp.cos(lam_im * delta)
    bar_im = mag * jnp.sin(lam_im * delta)
    den = lam_re * lam_re + lam_im * lam_im
    coef_re = ((bar_re - 1.0) * lam_re + bar_im * lam_im) / den
    coef_im = (bar_im * lam_re - (bar_re - 1.0) * lam_im) / den
    bb_re = coef_re[..., None] * b_re - coef_im[..., None] * b_im
    bb_im = coef_re[..., None] * b_im + coef_im[..., None] * b_re
    eye = jnp.eye(S, dtype=F32)

    def in_blocks(m):
        return jnp.einsum('kgpc,gh->kgchp', m.reshape(n_slab, S, P, C), eye).reshape(n_slab, S * C, S * P)

    def out_blocks(m):
        return jnp.einsum('kgcp,gh->kgphc', m.reshape(n_slab, S, C, P), eye).reshape(n_slab, S * P, S * C)

    bm = jnp.concatenate([in_blocks(bb_re), in_blocks(bb_im)], axis=2)
    cm = jnp.concatenate([out_blocks(c_re), -out_blocks(c_im)], axis=1)
    a_re = bar_re.reshape(1, G * P)
    a_im = bar_im.reshape(1, G * P)
    return bm.astype(BF16), cm.astype(BF16), a_re, a_im


def _even_w_in_layout(w):
    n_front = A_WIDTH + 2 * A_HEAD_DIM + IDX_HEADS * IDX_DIM + IDX_DIM + IDX_HEADS
    pad = jnp.zeros((w.shape[0], 128 - IDX_DIM - IDX_HEADS), w.dtype)
    return jnp.concatenate([w[:, :n_front], pad, w[:, n_front:]], axis=1).astype(BF16)


def kernel(x, mem, norm_mix, norm_x, norm_mem, norm_ffn, even_w_in, even_w_out, a_q_norm, a_k_norm,
           s5_lam_re, s5_lam_im, s5_log_step, s5_b_re, s5_b_im, s5_c_re, s5_c_im, s5_d, s5_w_glu, s5_b_glu,
           odd_w_in, odd_conv_w, odd_conv_b, odd_w_out, x_w_q, x_w_kv, x_w_o, x_q_norm, x_k_norm,
           f_w_up, f_conv_w, f_conv_b, f_w_down):
    B, T, D = x.shape
    M = mem.shape[1]
    depth = norm_mix.shape[0]
    assert D == 1024 and T % KEY_SUPER == 0 and even_w_in.shape[2] == 1476
    tm_proj, tm_fused, s5_steps = 512, 256, 64

    x2 = x.reshape(B * T, D)
    mem2 = mem.reshape(B * M, D)

    k_all, v_all = _mem_kv(mem2, norm_mem.reshape(depth, 1, D), x_w_kv.astype(BF16),
                           x_k_norm.reshape(depth, 1, X_HEAD_DIM))
    k_all = k_all.reshape(depth * B, M, X_WIDTH)
    v_all = v_all.reshape(depth * B, M, X_WIDTH)

    head_mean = jnp.kron(jnp.eye(A_HEADS, dtype=F32),
                         jnp.full((A_HEAD_DIM, A_HEAD_DIM), 1.0 / A_HEAD_DIM, F32)).astype(BF16)
    upper_tri = jnp.triu(jnp.ones((KEY_TILE, KEY_TILE), F32)).astype(BF16)

    for i in range(depth):
        j = i // 2
        g_mix = norm_mix[i].reshape(1, D)
        if i % 2 == 0:
            q, kv, iq, ikw, u_t = _even_in(x2, g_mix, _even_w_in_layout(even_w_in[j]), B, T, tm_proj)
            qg = jnp.tile(a_q_norm[j], A_HEADS).reshape(1, A_WIDTH)
            kg = jnp.concatenate([a_k_norm[j], jnp.zeros((128 - A_HEAD_DIM,), F32)]).reshape(1, 128)
            ya = _dsa(q, iq, ikw, kv, qg, kg, head_mean, upper_tri, B, T)
            bm, cm, a_re, a_im = _s5_operators(s5_lam_re[j], s5_lam_im[j], s5_log_step[j],
                                               s5_b_re[j], s5_b_im[j], s5_c_re[j], s5_c_im[j])
            yb_tm = _s5(u_t.reshape(T * B, 512), bm, cm, a_re, a_im, s5_d[j].reshape(1, 512),
                        s5_w_glu[j].astype(BF16), s5_b_glu[j].reshape(1, 512), B, T, s5_steps)
            w_out = even_w_out[j].astype(BF16)
            x2 = _even_out(x2, ya, yb_tm.reshape(T, B * 512), w_out[:A_WIDTH], w_out[A_WIDTH:],
                           B, T, tm_proj)
        else:
            x2 = _odd_mixer(x2, g_mix, odd_w_in[j].astype(BF16), odd_conv_w[j],
                            odd_conv_b[j].reshape(1, D), odd_w_out[j].astype(BF16), B, T, tm_fused)
        x2 = _xattn(x2, norm_x[i].reshape(1, D), x_w_q[i].astype(BF16), x_q_norm[i].reshape(1, X_HEAD_DIM),
                    k_all, v_all, x_w_o[i].astype(BF16), i, B, T, tm_proj, M)
        x2 = _ffn(x2, norm_ffn[i].reshape(1, D), f_w_up[i].astype(BF16), f_conv_w[i],
                  f_conv_b[i].reshape(1, -1), f_w_down[i].astype(BF16), B, T, tm_fused)
    return x2.reshape(B, T, D)
```

```python
import functools
import math

import jax
import jax.numpy as jnp
from jax import lax
from jax.experimental import pallas as pl
from jax.experimental.pallas import tpu as pltpu

F32 = jnp.float32
BF16 = jnp.bfloat16
I32 = jnp.int32

EPS = 1e-6
CHUNK = 64
DSA_Q_BLOCKS = (256, 512)
KEY_TILE = 128
TILES_PER_SUPER = 4
KEY_SUPER = KEY_TILE * TILES_PER_SUPER
KEY_BITS = 16
KEY_MIN = -(1 << (KEY_BITS - 1))
V_ROWS = 80
LOG2_E = 1.4426950408889634
SHIFT_SLACK = 1.02
MAX_SAFE_SHIFT = 50.0
A_HEADS = 8
A_HEAD_DIM = 64
A_WIDTH = A_HEADS * A_HEAD_DIM
IDX_HEADS = 4
IDX_DIM = 64
TOPK_MAX = 256
S5_GROUPS_PER_SLAB = 8
X_HEADS = 4
X_HEAD_DIM = 128
X_WIDTH = X_HEADS * X_HEAD_DIM
CONV_HALO = 8
MLP_CHUNK = 256
ODD_CHUNK = 256

V7X_VMEM_LIMIT_BYTES = 56 * 1024 * 1024
NEG_INF = float("-inf")


def _cparams(n_axes):
    return pltpu.CompilerParams(
        dimension_semantics=("arbitrary",) * n_axes,
        vmem_limit_bytes=V7X_VMEM_LIMIT_BYTES)


def _rms_rows(xf, g):
    ms = jnp.mean(xf * xf, axis=-1, keepdims=True)
    return xf * lax.rsqrt(ms + EPS) * g


def _dot(a, b):
    return jnp.dot(a, b, preferred_element_type=F32)


def _dot_nt(a, b):
    return lax.dot_general(a, b, (((1,), (1,)), ((), ())), preferred_element_type=F32)


def _resident(shape):
    nd = len(shape)
    return pl.BlockSpec(shape, lambda *_: (0,) * nd)


def _even_in_kernel(x_ref, g_ref, w_ref, q_ref, kv_ref, iq_ref, ikw_ref, u_ref):
    hn = _rms_rows(x_ref[...], g_ref[...])
    y = _dot(hn.astype(BF16), w_ref[...])
    q_ref[...] = y[:, 0:512].astype(BF16)
    kv_ref[...] = y[:, 512:640].astype(BF16)
    iq_ref[...] = y[:, 640:896].astype(BF16)
    ikw_ref[...] = y[:, 896:1024]
    u_ref[...] = y[:, 1024:1536].astype(BF16)


def _even_in(x2, g, w, B, T, tm):
    N, D = x2.shape
    nt = T // tm
    row = lambda b, t: (b * nt + t, 0)
    return pl.pallas_call(
        _even_in_kernel,
        grid=(B, nt),
        in_specs=[pl.BlockSpec((tm, D), row), _resident((1, D)), _resident(w.shape)],
        out_specs=[pl.BlockSpec((tm, 512), row), pl.BlockSpec((tm, 128), row),
                   pl.BlockSpec((tm, 256), row), pl.BlockSpec((tm, 128), row),
                   pl.BlockSpec((tm, 512), lambda b, t: (t, b))],
        out_shape=[jax.ShapeDtypeStruct((N, 512), BF16), jax.ShapeDtypeStruct((N, 128), BF16),
                   jax.ShapeDtypeStruct((N, 256), BF16), jax.ShapeDtypeStruct((N, 128), F32),
                   jax.ShapeDtypeStruct((T, B * 512), BF16)],
        compiler_params=_cparams(2),
        name="even_in",
    )(x2, g, w)


def _dsa_kernel(q_ref, iq_ref, iwq_ref, kv_ref, ikf_ref, qg_ref, kg_ref, bd_ref, ut_ref, o_ref,
                k1_s, ik2_s, vt_s, qs_s, iqs_s, wt_s, key_s, bias_s, m_s, acc_s, *, topk, QB):
    T = kv_ref.shape[0]
    qb = pl.program_id(1)
    nst = ((qb + 1) * (QB // KEY_TILE) + TILES_PER_SUPER - 1) >> (TILES_PER_SUPER.bit_length() - 1)

    qlane = lax.broadcasted_iota(I32, (KEY_TILE, QB), 1)
    row = lax.broadcasted_iota(I32, (KEY_TILE, QB), 0)
    lane = lax.broadcasted_iota(I32, (KEY_TILE, 128), 1)
    lo_half = lane < A_HEAD_DIM
    shift_lane = lane == A_HEAD_DIM

    q_scale = (A_HEAD_DIM ** -0.5) * LOG2_E
    shift = (A_HEAD_DIM * q_scale * SHIFT_SLACK) * jnp.max(jnp.abs(qg_ref[...])) * jnp.max(jnp.abs(kg_ref[...]))
    shift_is_safe = shift <= MAX_SAFE_SHIFT

    @pl.when(qb == 0)
    def _prepare_batch():
        ones_rows = jnp.where(lax.broadcasted_iota(I32, (V_ROWS - A_HEAD_DIM, KEY_TILE), 0) == 0, 1.0, 0.0)

        def key_tile(st, _):
            for t in range(TILES_PER_SUPER):
                off = pl.multiple_of(st * KEY_SUPER + t * KEY_TILE, KEY_TILE)
                kv = kv_ref[pl.ds(off, KEY_TILE), :].astype(F32)
                k = jnp.where(lo_half, kv, 0.0)
                ms = jnp.sum(k * k, axis=-1, keepdims=True) * (1.0 / A_HEAD_DIM)
                kn = k * lax.rsqrt(ms + EPS) * kg_ref[...]
                k1_s[pl.ds(off, KEY_TILE), :] = jnp.where(shift_lane, 1.0, kn).astype(BF16)
                ik = jnp.where(lo_half, ikf_ref[pl.ds(off, KEY_TILE), :], 0.0)
                ik2_s[pl.ds(off, KEY_TILE), :] = (ik + pltpu.roll(ik, 64, 1)).astype(BF16)
                v = pltpu.roll(jnp.where(lo_half, 0.0, kv), 64, 1)
                vt1 = jnp.concatenate([v.T[0:A_HEAD_DIM, :], ones_rows], axis=0)
                vt_s[st, :, t * KEY_TILE:(t + 1) * KEY_TILE] = vt1.astype(BF16)
            return 0

        lax.fori_loop(0, T // KEY_SUPER, key_tile, 0)

    q = q_ref[...].astype(F32)
    ms = _dot((q * q).astype(BF16), bd_ref[...])
    qn = q * lax.rsqrt(ms + EPS) * qg_ref[...] * q_scale
    half_q = lax.broadcasted_iota(I32, (QB, 128), 1) < A_HEAD_DIM
    neg_shift_lane = jnp.where(lax.broadcasted_iota(I32, (QB, 128), 1) == A_HEAD_DIM, -1.0, 0.0) * shift
    for h in range(A_HEADS):
        blk = qn[:, 128 * (h // 2):128 * (h // 2) + 128]
        if h % 2 == 1:
            blk = pltpu.roll(blk, 64, 1)
        qs_s[h * QB:(h + 1) * QB, :] = (jnp.where(half_q, blk, 0.0) + neg_shift_lane).astype(BF16)
    iq = iq_ref[...].astype(F32)
    for h in range(IDX_HEADS):
        blk = iq[:, 128 * (h // 2):128 * (h // 2) + 128]
        keep = half_q if h % 2 == 0 else jnp.logical_not(half_q)
        iqs_s[h * QB:(h + 1) * QB, :] = jnp.where(keep, blk, 0.0).astype(BF16)
    wt_s[...] = iwq_ref[...].T * ((IDX_DIM ** -0.5) * (IDX_HEADS ** -0.5))

    chunk_shift = CHUNK.bit_length() - 1
    limit = (((qb * QB + qlane) >> chunk_shift) + 1) << chunk_shift

    def score_tile(st, _):
        off = pl.multiple_of(st * KEY_SUPER, KEY_SUPER)
        x = _dot_nt(ik2_s[pl.ds(off, KEY_SUPER), :], iqs_s[...])
        for t in range(TILES_PER_SUPER):
            rows = slice(t * KEY_TILE, (t + 1) * KEY_TILE)
            sc = wt_s[64:65, :] * jnp.maximum(x[rows, 0:QB], 0.0)
            for h in range(1, IDX_HEADS):
                sc = sc + wt_s[64 + h:65 + h, :] * jnp.maximum(x[rows, QB * h:QB * (h + 1)], 0.0)
            sc = jnp.where(sc == 0.0, 0.0, sc)
            sc = jnp.where(off + t * KEY_TILE + row < limit, sc, NEG_INF)
            bits = pltpu.bitcast(sc.astype(BF16).astype(F32), I32) >> 16
            key_s[pl.ds(off + t * KEY_TILE, KEY_TILE), :] = bits ^ ((bits >> 31) & 0x7FFF)
        return 0

    lax.fori_loop(0, nst, score_tile, 0)

    def count_ge(cand):
        def body(st, acc):
            off = pl.multiple_of(st * KEY_SUPER, KEY_SUPER)
            parts = []
            for t in range(TILES_PER_SUPER):
                ge = jnp.where(key_s[pl.ds(off + t * KEY_TILE, KEY_TILE), :] >= cand, 1, 0)
                parts.append(jnp.sum(ge.reshape(KEY_TILE // 8, 8, QB), axis=0))
            return acc + ((parts[0] + parts[1]) + (parts[2] + parts[3]))
        acc = lax.fori_loop(0, nst, body, jnp.zeros((8, QB), I32))
        return jnp.sum(acc, axis=0, keepdims=True)

    def bit_step(i, carry):
        prefix, cnt_ge = carry
        cand_u = prefix | lax.shift_left(jnp.int32(1), KEY_BITS - 1 - i)
        cnt = count_ge(cand_u + KEY_MIN)
        ok = cnt >= topk
        return jnp.where(ok, cand_u, prefix), jnp.where(ok, cnt, cnt_ge)

    prefix, cnt_ge = lax.fori_loop(
        0, KEY_BITS, bit_step,
        (jnp.zeros((1, QB), I32), jnp.zeros((1, QB), I32) + nst * KEY_SUPER))
    thresh = prefix + KEY_MIN
    excess = (cnt_ge - topk).astype(F32)

    def mask_tile(i, later_ties):
        off = pl.multiple_of((nst - 1 - i) * KEY_SUPER, KEY_SUPER)
        for t in reversed(range(TILES_PER_SUPER)):
            o = off + t * KEY_TILE
            key = key_s[pl.ds(o, KEY_TILE), :]
            eq = key == thresh
            eq_f = jnp.where(eq, 1.0, 0.0)
            suffix = later_ties + _dot(ut_ref[...], eq_f.astype(BF16))
            tie_bias = jnp.where(suffix > excess, 0.0, NEG_INF)
            bias = jnp.where(key > thresh, 0.0, jnp.where(eq, tie_bias, NEG_INF))
            bias_s[pl.ds(o, KEY_TILE), :] = jnp.where(o + row < limit, bias, NEG_INF)
            later_ties = later_ties + jnp.sum(eq_f, axis=0, keepdims=True)
        return later_ties

    lax.fori_loop(0, nst, mask_tile, jnp.zeros((1, QB), F32))

    acc_s[...] = jnp.zeros(acc_s.shape, F32)

    @pl.when(shift_is_safe)
    def _attend_shifted():
        def scores(st):
            off = pl.multiple_of(st * KEY_SUPER, KEY_SUPER)
            return _dot_nt(k1_s[pl.ds(off, KEY_SUPER), :], qs_s[...])

        def attend(st, sp):
            off = pl.multiple_of(st * KEY_SUPER, KEY_SUPER)
            bias = bias_s[pl.ds(off, KEY_SUPER), :]
            p = jnp.concatenate(
                [jnp.exp2(sp[:, h * QB:(h + 1) * QB] + bias).astype(BF16) for h in range(A_HEADS)],
                axis=1)
            acc_s[...] += _dot(vt_s[st], p)

        def attend_pair(i, _):
            sp0 = scores(2 * i)
            sp1 = scores(2 * i + 1)
            attend(2 * i, sp0)
            attend(2 * i + 1, sp1)
            return 0

        lax.fori_loop(0, nst >> 1, attend_pair, 0)

        @pl.when((nst & 1) == 1)
        def _odd_tail():
            attend(nst - 1, scores(nst - 1))

    @pl.when(jnp.logical_not(shift_is_safe))
    def _attend_online():
        m_s[...] = jnp.full(m_s.shape, NEG_INF, F32)

        def attend_tile(st, _):
            off = pl.multiple_of(st * KEY_SUPER, KEY_SUPER)
            k_tile = k1_s[pl.ds(off, KEY_SUPER), :]
            bias = bias_s[pl.ds(off, KEY_SUPER), :]
            for h in range(A_HEADS):
                cols = slice(h * QB, (h + 1) * QB)
                s = _dot_nt(k_tile, qs_s[cols, :]) + bias
                m_old = m_s[h:h + 1, :]
                m_new = jnp.maximum(m_old, jnp.max(s, axis=0, keepdims=True))
                m_safe = jnp.where(m_new == NEG_INF, 0.0, m_new)
                p = jnp.exp2(s - m_safe).astype(BF16)
                acc_s[:, cols] = jnp.exp2(m_old - m_safe) * acc_s[:, cols] + _dot(vt_s[st], p)
                m_s[h:h + 1, :] = m_new
            return 0

        lax.fori_loop(0, nst, attend_tile, 0)

    out_t = jnp.concatenate(
        [acc_s[0:A_HEAD_DIM, h * QB:(h + 1) * QB] / acc_s[A_HEAD_DIM:A_HEAD_DIM + 1, h * QB:(h + 1) * QB]
         for h in range(A_HEADS)], axis=0)
    o_ref[...] = out_t.T.astype(BF16)


def _dsa(q, iq, ikw, kv, qg, kg, bd, ut, B, T, QB):
    N = q.shape[0]
    nqb = T // QB
    topk = min(TOPK_MAX, T // 4)
    qrow = lambda b, j: (b * nqb + j, 0)
    brow = lambda b, j: (b, 0)
    return pl.pallas_call(
        functools.partial(_dsa_kernel, topk=topk, QB=QB),
        grid=(B, nqb),
        in_specs=[pl.BlockSpec((QB, 512), qrow), pl.BlockSpec((QB, 256), qrow),
                  pl.BlockSpec((QB, 128), qrow), pl.BlockSpec((T, 128), brow),
                  pl.BlockSpec((T, 128), brow), _resident((1, 512)), _resident((1, 128)),
                  _resident((512, 512)), _resident((KEY_TILE, KEY_TILE))],
        out_specs=pl.BlockSpec((QB, 512), qrow),
        out_shape=jax.ShapeDtypeStruct((N, 512), BF16),
        scratch_shapes=[
            pltpu.VMEM((T, 128), BF16),
            pltpu.VMEM((T, 128), BF16),
            pltpu.VMEM((T // KEY_SUPER, V_ROWS, KEY_SUPER), BF16),
            pltpu.VMEM((A_HEADS * QB, 128), BF16),
            pltpu.VMEM((IDX_HEADS * QB, 128), BF16),
            pltpu.VMEM((128, QB), F32),
            pltpu.VMEM((T, QB), I32),
            pltpu.VMEM((T, QB), F32),
            pltpu.VMEM((A_HEADS, QB), F32),
            pltpu.VMEM((V_ROWS, A_HEADS * QB), F32),
        ],
        compiler_params=_cparams(2),
        name="dsa_attention",
    )(q, iq, ikw, kv, ikw, qg, kg, bd, ut)


def _gelu_tanh(x):
    return 0.5 * x * (1.0 + jnp.tanh(math.sqrt(2.0 / math.pi) * (x + 0.044715 * (x * x * x))))


def _s5_kernel(u_ref, bm_ref, cm_ref, are_ref, aim_ref, d_ref, wg_ref, bg_ref, o_ref, buf_s, h_s,
               *, B, steps):
    n_slab = bm_ref.shape[0]
    half = bm_ref.shape[2] // 2
    width = 2 * half

    @pl.when(pl.program_id(0) == 0)
    def _init():
        h_s[...] = jnp.zeros(h_s.shape, F32)

    u = u_ref[...]

    def expand(k):
        buf_s[:, width * k:width * (k + 1)] = _dot(u[:, 128 * k:128 * (k + 1)], bm_ref[k])

    ys = []
    expand(0)
    for k in range(n_slab):
        if k + 1 < n_slab:
            expand(k + 1)
        c0 = width * k
        ar = jnp.broadcast_to(are_ref[:, half * k:half * (k + 1)], (B, half))
        ai = jnp.broadcast_to(aim_ref[:, half * k:half * (k + 1)], (B, half))
        hr, hi = h_s[:, c0:c0 + half], h_s[:, c0 + half:c0 + width]
        for t in range(steps):
            rows = slice(t * B, (t + 1) * B)
            hr, hi = (ar * hr - ai * hi + buf_s[rows, c0:c0 + half],
                      ar * hi + ai * hr + buf_s[rows, c0 + half:c0 + width])
            buf_s[rows, c0:c0 + half] = hr
            buf_s[rows, c0 + half:c0 + width] = hi
        h_s[:, c0:c0 + half] = hr
        h_s[:, c0 + half:c0 + width] = hi
        ys.append(_dot(buf_s[:, c0:c0 + width].astype(BF16), cm_ref[k]))
    y = jnp.concatenate(ys, axis=1)
    y = _gelu_tanh(y + d_ref[...] * u.astype(F32))
    z = _dot(y.astype(BF16), wg_ref[...]) + bg_ref[...]
    o_ref[...] = (y * (1.0 / (1.0 + jnp.exp(-z)))).astype(BF16)


def _s5(u_tm, bm, cm, a_re, a_im, d, wg, bg, B, T, steps):
    rows = steps * B
    n_state = bm.shape[0] * bm.shape[2]
    return pl.pallas_call(
        functools.partial(_s5_kernel, B=B, steps=steps),
        grid=(T // steps,),
        in_specs=[pl.BlockSpec((rows, 512), lambda i: (i, 0)), _resident(bm.shape), _resident(cm.shape),
                  _resident(a_re.shape), _resident(a_im.shape), _resident(d.shape),
                  _resident(wg.shape), _resident(bg.shape)],
        out_specs=pl.BlockSpec((rows, 512), lambda i: (i, 0)),
        out_shape=jax.ShapeDtypeStruct((T * B, 512), BF16),
        scratch_shapes=[pltpu.VMEM((rows, n_state), F32),
                        pltpu.VMEM((B, n_state), F32)],
        compiler_params=_cparams(1),
        name="s5_scan",
    )(u_tm, bm, cm, a_re, a_im, d, wg, bg)


def _reset_halo(halo_s, first_tile):
    @pl.when(first_tile)
    def _zero_halo():
        halo_s[0:CONV_HALO, :] = jnp.zeros((CONV_HALO, halo_s.shape[1]), F32)


def _causal_conv3(c, halo_s, w_ref, b_ref, cols):
    tm = c.shape[0]
    halo_s[CONV_HALO:CONV_HALO + tm, cols] = c
    c1 = halo_s[CONV_HALO - 1:CONV_HALO - 1 + tm, cols]
    c2 = halo_s[CONV_HALO - 2:CONV_HALO - 2 + tm, cols]
    y = w_ref[0:1, cols] * c2 + w_ref[1:2, cols] * c1 + w_ref[2:3, cols] * c + b_ref[:, cols]
    halo_s[0:CONV_HALO, cols] = halo_s[tm:tm + CONV_HALO, cols]
    return y


def _odd_kernel(x_ref, g_ref, wi_ref, cw_ref, cb_ref, wo_ref, o_ref, halo_s):
    D = x_ref.shape[1]
    x = x_ref[...]
    hn = _rms_rows(x, g_ref[...]).astype(BF16)
    _reset_halo(halo_s, pl.program_id(1) == 0)
    acc = x
    starts = list(range(0, D, ODD_CHUNK))

    def in_proj(c0):
        return tuple(_dot(hn, wi_ref[:, part * D + c0:part * D + c0 + ODD_CHUNK]) for part in range(3))

    nxt = in_proj(0)
    for n, c0 in enumerate(starts):
        gb, gc, z = nxt
        if n + 1 < len(starts):
            nxt = in_proj(starts[n + 1])
        cols = slice(c0, c0 + ODD_CHUNK)
        conv = _causal_conv3(gc * z, halo_s, cw_ref, cb_ref, cols)
        acc = acc + _dot((gb * conv).astype(BF16), wo_ref[cols, :])
    o_ref[...] = acc


def _odd_mixer(x2, g, wi, cw, cb, wo, B, T, tm):
    N, D = x2.shape
    nt = T // tm
    row = lambda b, t: (b * nt + t, 0)
    return pl.pallas_call(
        _odd_kernel,
        grid=(B, nt),
        in_specs=[pl.BlockSpec((tm, D), row), _resident((1, D)), _resident(wi.shape),
                  _resident(cw.shape), _resident(cb.shape), _resident(wo.shape)],
        out_specs=pl.BlockSpec((tm, D), row),
        out_shape=jax.ShapeDtypeStruct((N, D), F32),
        scratch_shapes=[pltpu.VMEM((tm + CONV_HALO, D), F32)],
        compiler_params=_cparams(2),
        name="odd_mixer",
    )(x2, g, wi, cw, cb, wo)


def _mem_kv_kernel(m_ref, g_ref, w_ref, kg_ref, k_ref, v_ref):
    hn = _rms_rows(m_ref[...], g_ref[0])
    y = _dot(hn.astype(BF16), w_ref[0])
    ks = []
    for h in range(X_HEADS):
        ks.append(_rms_rows(y[:, X_HEAD_DIM * h:X_HEAD_DIM * (h + 1)], kg_ref[0]))
    k_ref[0] = jnp.concatenate(ks, axis=1).astype(BF16)
    v_ref[0] = y[:, X_WIDTH:2 * X_WIDTH].astype(BF16)


def _mem_kv(mem2, g, w, kg):
    depth = w.shape[0]
    NM, D = mem2.shape
    return pl.pallas_call(
        _mem_kv_kernel,
        grid=(depth,),
        in_specs=[_resident((NM, D)), pl.BlockSpec((1, 1, D), lambda i: (i, 0, 0)),
                  pl.BlockSpec((1, D, 2 * X_WIDTH), lambda i: (i, 0, 0)),
                  pl.BlockSpec((1, 1, X_HEAD_DIM), lambda i: (i, 0, 0))],
        out_specs=[pl.BlockSpec((1, NM, X_WIDTH), lambda i: (i, 0, 0)),
                   pl.BlockSpec((1, NM, X_WIDTH), lambda i: (i, 0, 0))],
        out_shape=[jax.ShapeDtypeStruct((depth, NM, X_WIDTH), BF16)] * 2,
        compiler_params=_cparams(1),
        name="mem_kv",
    )(mem2, g, w, kg)


def _xattn_kernel(*refs, mixer_out):
    if mixer_out:
        x_ref, ya_ref, yb_ref, wa_ref, wb_ref, g_ref, wq_ref, qg_ref, k_ref, v_ref, wo_ref, o_ref = refs
        x = x_ref[...] + _dot(ya_ref[...], wa_ref[...]) + _dot(yb_ref[...], wb_ref[...])
    else:
        x_ref, g_ref, wq_ref, qg_ref, k_ref, v_ref, wo_ref, o_ref = refs
        x = x_ref[...]
    hn = _rms_rows(x, g_ref[...])
    q = _dot(hn.astype(BF16), wq_ref[...])
    k = k_ref[0]
    v = v_ref[0]
    heads = [slice(X_HEAD_DIM * h, X_HEAD_DIM * (h + 1)) for h in range(X_HEADS)]
    scores = [_dot_nt((_rms_rows(q[:, sl], qg_ref[...]) * (X_HEAD_DIM ** -0.5)).astype(BF16), k[:, sl])
              for sl in heads]
    outs = []
    for sl, s in zip(heads, scores):
        p = jnp.exp(s - jnp.max(s, axis=-1, keepdims=True))
        outs.append(_dot(p.astype(BF16), v[:, sl]) / jnp.sum(p, axis=-1, keepdims=True))
    o_ref[...] = x + _dot(jnp.concatenate(outs, axis=1).astype(BF16), wo_ref[...])


def _xattn(x2, mixer_out, g, wq, qg, k_all, v_all, wo, layer, B, T, tm, M):
    N, D = x2.shape
    nt = T // tm
    row = lambda b, t: (b * nt + t, 0)
    mrow = lambda b, t: (layer * B + b, 0, 0)
    mixer_args, mixer_specs = (), []
    if mixer_out is not None:
        ya, yb_t, wa, wb = mixer_out
        mixer_args = (ya, yb_t, wa, wb)
        mixer_specs = [pl.BlockSpec((tm, ya.shape[1]), row), pl.BlockSpec((tm, wb.shape[0]), lambda b, t: (t, b)),
                       _resident(wa.shape), _resident(wb.shape)]
    return pl.pallas_call(
        functools.partial(_xattn_kernel, mixer_out=mixer_out is not None),
        grid=(B, nt),
        in_specs=[pl.BlockSpec((tm, D), row)] + mixer_specs + [
            _resident((1, D)), _resident(wq.shape), _resident((1, X_HEAD_DIM)),
            pl.BlockSpec((1, M, X_WIDTH), mrow), pl.BlockSpec((1, M, X_WIDTH), mrow), _resident(wo.shape)],
        out_specs=pl.BlockSpec((tm, D), row),
        out_shape=jax.ShapeDtypeStruct((N, D), F32),
        compiler_params=_cparams(2),
        name="mem_xattn",
    )(x2, *mixer_args, g, wq, qg, k_all, v_all, wo)


def _ffn_kernel(x_ref, g_ref, wu_ref, cw_ref, cb_ref, wd_ref, o_ref, halo_s):
    F = cw_ref.shape[1]
    x = x_ref[...]
    hn = _rms_rows(x, g_ref[...]).astype(BF16)
    _reset_halo(halo_s, pl.program_id(1) == 0)
    acc = x
    starts = list(range(0, F, MLP_CHUNK))

    def up_proj(c0):
        return _dot(hn, wu_ref[:, c0:c0 + MLP_CHUNK]), _dot(hn, wu_ref[:, F + c0:F + c0 + MLP_CHUNK])

    nxt = up_proj(0)
    for n, c0 in enumerate(starts):
        gate, up = nxt
        if n + 1 < len(starts):
            nxt = up_proj(starts[n + 1])
        cols = slice(c0, c0 + MLP_CHUNK)
        conv = _causal_conv3(gate, halo_s, cw_ref, cb_ref, cols)
        act = conv * (1.0 / (1.0 + jnp.exp(-conv))) * up
        acc = acc + _dot(act.astype(BF16), wd_ref[cols, :])
    o_ref[...] = acc


def _ffn(x2, g, wu, cw, cb, wd, B, T, tm):
    N, D = x2.shape
    F = cw.shape[1]
    nt = T // tm
    row = lambda b, t: (b * nt + t, 0)
    return pl.pallas_call(
        _ffn_kernel,
        grid=(B, nt),
        in_specs=[pl.BlockSpec((tm, D), row), _resident((1, D)),
                  pl.BlockSpec(wu.shape, lambda b, t: (0, 0), pipeline_mode=pl.Buffered(1)),
                  _resident(cw.shape), _resident(cb.shape),
                  pl.BlockSpec(wd.shape, lambda b, t: (0, 0), pipeline_mode=pl.Buffered(1))],
        out_specs=pl.BlockSpec((tm, D), row),
        out_shape=jax.ShapeDtypeStruct((N, D), F32),
        scratch_shapes=[pltpu.VMEM((tm + CONV_HALO, F), F32)],
        compiler_params=_cparams(2),
        name="conv_glu_ffn",
    )(x2, g, wu, cw, cb, wd)


def _s5_operators(lam_re, lam_im, log_step, b_re, b_im, c_re, c_im):
    G, P = lam_re.shape
    C = b_re.shape[-1]
    S = S5_GROUPS_PER_SLAB
    n_slab = G // S
    delta = jnp.exp(log_step)[:, None]
    mag = jnp.exp(lam_re * delta)
    bar_re = mag * jnp.cos(lam_im * delta)
    bar_im = mag * jnp.sin(lam_im * delta)
    den = lam_re * lam_re + lam_im * lam_im
    coef_re = ((bar_re - 1.0) * lam_re + bar_im * lam_im) / den
    coef_im = (bar_im * lam_re - (bar_re - 1.0) * lam_im) / den
    bb_re = coef_re[..., None] * b_re - coef_im[..., None] * b_im
    bb_im = coef_re[..., None] * b_im + coef_im[..., None] * b_re
    eye = jnp.eye(S, dtype=F32)

    def in_blocks(m):
        return jnp.einsum('kgpc,gh->kgchp', m.reshape(n_slab, S, P, C), eye).reshape(n_slab, S * C, S * P)

    def out_blocks(m):
        return jnp.einsum('kgcp,gh->kgphc', m.reshape(n_slab, S, C, P), eye).reshape(n_slab, S * P, S * C)

    bm = jnp.concatenate([in_blocks(bb_re), in_blocks(bb_im)], axis=2)
    cm = jnp.concatenate([out_blocks(c_re), -out_blocks(c_im)], axis=1)
    a_re = bar_re.reshape(1, G * P)
    a_im = bar_im.reshape(1, G * P)
    return bm.astype(BF16), cm.astype(BF16), a_re, a_im


def _even_w_in_layout(w):
    n_front = A_WIDTH + 2 * A_HEAD_DIM + IDX_HEADS * IDX_DIM + IDX_DIM + IDX_HEADS
    pad = jnp.zeros((w.shape[0], 128 - IDX_DIM - IDX_HEADS), w.dtype)
    return jnp.concatenate([w[:, :n_front], pad, w[:, n_front:]], axis=1).astype(BF16)


def kernel(x, mem, norm_mix, norm_x, norm_mem, norm_ffn, even_w_in, even_w_out, a_q_norm, a_k_norm,
           s5_lam_re, s5_lam_im, s5_log_step, s5_b_re, s5_b_im, s5_c_re, s5_c_im, s5_d, s5_w_glu, s5_b_glu,
           odd_w_in, odd_conv_w, odd_conv_b, odd_w_out, x_w_q, x_w_kv, x_w_o, x_q_norm, x_k_norm,
           f_w_up, f_conv_w, f_conv_b, f_w_down):
    B, T, D = x.shape
    M = mem.shape[1]
    depth = norm_mix.shape[0]
    assert D == 1024 and T % KEY_SUPER == 0 and even_w_in.shape[2] == 1476
    tm_proj, tm_fused, s5_steps = 512, 256, 64

    x2 = x.reshape(B * T, D)
    mem2 = mem.reshape(B * M, D)

    k_all, v_all = _mem_kv(mem2, norm_mem.reshape(depth, 1, D), x_w_kv.astype(BF16),
                           x_k_norm.reshape(depth, 1, X_HEAD_DIM))
    k_all = k_all.reshape(depth * B, M, X_WIDTH)
    v_all = v_all.reshape(depth * B, M, X_WIDTH)

    head_mean = jnp.kron(jnp.eye(A_HEADS, dtype=F32),
                         jnp.full((A_HEAD_DIM, A_HEAD_DIM), 1.0 / A_HEAD_DIM, F32)).astype(BF16)
    upper_tri = jnp.triu(jnp.ones((KEY_TILE, KEY_TILE), F32)).astype(BF16)

    for i in range(depth):
        j = i // 2
        g_mix = norm_mix[i].reshape(1, D)
        if i % 2 == 0:
            q, kv, iq, ikw, u_t = _even_in(x2, g_mix, _even_w_in_layout(even_w_in[j]), B, T, tm_proj)
            qg = jnp.tile(a_q_norm[j], A_HEADS).reshape(1, A_WIDTH)
            kg = jnp.concatenate([a_k_norm[j], jnp.zeros((128 - A_HEAD_DIM,), F32)]).reshape(1, 128)
            ya = _dsa(q, iq, ikw, kv, qg, kg, head_mean, upper_tri, B, T, DSA_Q_BLOCKS[j % 2])
            bm, cm, a_re, a_im = _s5_operators(s5_lam_re[j], s5_lam_im[j], s5_log_step[j],
                                               s5_b_re[j], s5_b_im[j], s5_c_re[j], s5_c_im[j])
            yb_tm = _s5(u_t.reshape(T * B, 512), bm, cm, a_re, a_im, s5_d[j].reshape(1, 512),
                        s5_w_glu[j].astype(BF16), s5_b_glu[j].reshape(1, 512), B, T, s5_steps)
            w_out = even_w_out[j].astype(BF16)
            mixer_out = (ya, yb_tm.reshape(T, B * 512), w_out[:A_WIDTH], w_out[A_WIDTH:])
        else:
            x2 = _odd_mixer(x2, g_mix, odd_w_in[j].astype(BF16), odd_conv_w[j],
                            odd_conv_b[j].reshape(1, D), odd_w_out[j].astype(BF16), B, T, tm_fused)
            mixer_out = None
        x2 = _xattn(x2, mixer_out, norm_x[i].reshape(1, D), x_w_q[i].astype(BF16),
                    x_q_norm[i].reshape(1, X_HEAD_DIM), k_all, v_all, x_w_o[i].astype(BF16), i, B, T, tm_proj, M)
        x2 = _ffn(x2, norm_ffn[i].reshape(1, D), f_w_up[i].astype(BF16), f_conv_w[i],
                  f_conv_b[i].reshape(1, -1), f_w_down[i].astype(BF16), B, T, tm_fused)
    return x2.reshape(B, T, D)
```

```python
import functools
import math

import jax
import jax.numpy as jnp
from jax import lax
from jax.experimental import pallas as pl
from jax.experimental.pallas import tpu as pltpu

F32 = jnp.float32
BF16 = jnp.bfloat16
I32 = jnp.int32

EPS = 1e-6
CHUNK = 64
Q_BLOCK = 512
KEY_TILE = 128
TILES_PER_SUPER = 4
KEY_SUPER = KEY_TILE * TILES_PER_SUPER
KEY_BITS = 16
KEY_MIN = -(1 << (KEY_BITS - 1))
V_ROWS = 80
LOG2_E = 1.4426950408889634
SHIFT_SLACK = 1.02
MAX_SAFE_SHIFT = 50.0
A_HEADS = 8
A_HEAD_DIM = 64
A_WIDTH = A_HEADS * A_HEAD_DIM
IDX_HEADS = 4
IDX_DIM = 64
TOPK_MAX = 256
S5_GROUPS_PER_SLAB = 8
X_HEADS = 4
X_HEAD_DIM = 128
X_WIDTH = X_HEADS * X_HEAD_DIM
CONV_HALO = 8
MLP_CHUNK = 256
ODD_CHUNK = 256

V7X_VMEM_LIMIT_BYTES = 56 * 1024 * 1024
NEG_INF = float("-inf")


def _cparams(n_axes):
    return pltpu.CompilerParams(
        dimension_semantics=("arbitrary",) * n_axes,
        vmem_limit_bytes=V7X_VMEM_LIMIT_BYTES)


def _rms_rows(xf, g):
    ms = jnp.mean(xf * xf, axis=-1, keepdims=True)
    return xf * lax.rsqrt(ms + EPS) * g


def _dot(a, b):
    return jnp.dot(a, b, preferred_element_type=F32)


def _dot_nt(a, b):
    return lax.dot_general(a, b, (((1,), (1,)), ((), ())), preferred_element_type=F32)


def _resident(shape):
    nd = len(shape)
    return pl.BlockSpec(shape, lambda *_: (0,) * nd)


def _even_in_kernel(x_ref, g_ref, w_ref, q_ref, kv_ref, iq_ref, ikw_ref, u_ref):
    hn = _rms_rows(x_ref[...], g_ref[...])
    y = _dot(hn.astype(BF16), w_ref[...])
    q_ref[...] = y[:, 0:512].astype(BF16)
    kv_ref[...] = y[:, 512:640].astype(BF16)
    iq_ref[...] = y[:, 640:896].astype(BF16)
    ikw_ref[...] = y[:, 896:1024]
    u_ref[...] = y[:, 1024:1536].astype(BF16)


def _even_in(x2, g, w, B, T, tm):
    N, D = x2.shape
    nt = T // tm
    row = lambda b, t: (b * nt + t, 0)
    return pl.pallas_call(
        _even_in_kernel,
        grid=(B, nt),
        in_specs=[pl.BlockSpec((tm, D), row), _resident((1, D)), _resident(w.shape)],
        out_specs=[pl.BlockSpec((tm, 512), row), pl.BlockSpec((tm, 128), row),
                   pl.BlockSpec((tm, 256), row), pl.BlockSpec((tm, 128), row),
                   pl.BlockSpec((tm, 512), lambda b, t: (t, b))],
        out_shape=[jax.ShapeDtypeStruct((N, 512), BF16), jax.ShapeDtypeStruct((N, 128), BF16),
                   jax.ShapeDtypeStruct((N, 256), BF16), jax.ShapeDtypeStruct((N, 128), F32),
                   jax.ShapeDtypeStruct((T, B * 512), BF16)],
        compiler_params=_cparams(2),
        name="even_in",
    )(x2, g, w)


def _dsa_kernel(q_ref, iq_ref, iwq_ref, kv_ref, ikf_ref, qg_ref, kg_ref, bd_ref, ut_ref, o_ref,
                k1_s, ik2_s, vt_s, qs_s, iqs_s, wt_s, key_s, bias_s, m_s, acc_s, *, topk, QB):
    T = kv_ref.shape[0]
    qb = pl.program_id(1)
    nst = ((qb + 1) * (QB // KEY_TILE) + TILES_PER_SUPER - 1) >> (TILES_PER_SUPER.bit_length() - 1)

    qlane = lax.broadcasted_iota(I32, (KEY_TILE, QB), 1)
    row = lax.broadcasted_iota(I32, (KEY_TILE, QB), 0)
    lane = lax.broadcasted_iota(I32, (KEY_TILE, 128), 1)
    lo_half = lane < A_HEAD_DIM
    shift_lane = lane == A_HEAD_DIM

    q_scale = (A_HEAD_DIM ** -0.5) * LOG2_E
    shift = (A_HEAD_DIM * q_scale * SHIFT_SLACK) * jnp.max(jnp.abs(qg_ref[...])) * jnp.max(jnp.abs(kg_ref[...]))
    shift_is_safe = shift <= MAX_SAFE_SHIFT

    @pl.when(qb == 0)
    def _prepare_batch():
        ones_rows = jnp.where(lax.broadcasted_iota(I32, (V_ROWS - A_HEAD_DIM, KEY_TILE), 0) == 0, 1.0, 0.0)

        def key_tile(st, _):
            for t in range(TILES_PER_SUPER):
                off = pl.multiple_of(st * KEY_SUPER + t * KEY_TILE, KEY_TILE)
                kv = kv_ref[pl.ds(off, KEY_TILE), :].astype(F32)
                k = jnp.where(lo_half, kv, 0.0)
                ms = jnp.sum(k * k, axis=-1, keepdims=True) * (1.0 / A_HEAD_DIM)
                kn = k * lax.rsqrt(ms + EPS) * kg_ref[...]
                k1_s[pl.ds(off, KEY_TILE), :] = jnp.where(shift_lane, 1.0, kn).astype(BF16)
                ik = jnp.where(lo_half, ikf_ref[pl.ds(off, KEY_TILE), :], 0.0)
                ik2_s[pl.ds(off, KEY_TILE), :] = (ik + pltpu.roll(ik, 64, 1)).astype(BF16)
                v = pltpu.roll(jnp.where(lo_half, 0.0, kv), 64, 1)
                vt1 = jnp.concatenate([v.T[0:A_HEAD_DIM, :], ones_rows], axis=0)
                vt_s[st, :, t * KEY_TILE:(t + 1) * KEY_TILE] = vt1.astype(BF16)
            return 0

        lax.fori_loop(0, T // KEY_SUPER, key_tile, 0)

    q = q_ref[...].astype(F32)
    ms = _dot((q * q).astype(BF16), bd_ref[...])
    qn = q * lax.rsqrt(ms + EPS) * qg_ref[...] * q_scale
    half_q = lax.broadcasted_iota(I32, (QB, 128), 1) < A_HEAD_DIM
    neg_shift_lane = jnp.where(lax.broadcasted_iota(I32, (QB, 128), 1) == A_HEAD_DIM, -1.0, 0.0) * shift
    for h in range(A_HEADS):
        blk = qn[:, 128 * (h // 2):128 * (h // 2) + 128]
        if h % 2 == 1:
            blk = pltpu.roll(blk, 64, 1)
        qs_s[h * QB:(h + 1) * QB, :] = (jnp.where(half_q, blk, 0.0) + neg_shift_lane).astype(BF16)
    iq = iq_ref[...].astype(F32)
    for h in range(IDX_HEADS):
        blk = iq[:, 128 * (h // 2):128 * (h // 2) + 128]
        keep = half_q if h % 2 == 0 else jnp.logical_not(half_q)
        iqs_s[h * QB:(h + 1) * QB, :] = jnp.where(keep, blk, 0.0).astype(BF16)
    wt_s[...] = iwq_ref[...].T * ((IDX_DIM ** -0.5) * (IDX_HEADS ** -0.5))

    chunk_shift = CHUNK.bit_length() - 1
    limit = (((qb * QB + qlane) >> chunk_shift) + 1) << chunk_shift

    def score_tile(st, _):
        off = pl.multiple_of(st * KEY_SUPER, KEY_SUPER)
        x = _dot_nt(ik2_s[pl.ds(off, KEY_SUPER), :], iqs_s[...])
        for t in range(TILES_PER_SUPER):
            rows = slice(t * KEY_TILE, (t + 1) * KEY_TILE)
            sc = wt_s[64:65, :] * jnp.maximum(x[rows, 0:QB], 0.0)
            for h in range(1, IDX_HEADS):
                sc = sc + wt_s[64 + h:65 + h, :] * jnp.maximum(x[rows, QB * h:QB * (h + 1)], 0.0)
            sc = jnp.where(sc == 0.0, 0.0, sc)
            sc = jnp.where(off + t * KEY_TILE + row < limit, sc, NEG_INF)
            bits = pltpu.bitcast(sc.astype(BF16).astype(F32), I32) >> 16
            key_s[pl.ds(off + t * KEY_TILE, KEY_TILE), :] = bits ^ ((bits >> 31) & 0x7FFF)
        return 0

    lax.fori_loop(0, nst, score_tile, 0)

    def count_ge(cand):
        def body(st, acc):
            off = pl.multiple_of(st * KEY_SUPER, KEY_SUPER)
            parts = []
            for t in range(TILES_PER_SUPER):
                ge = jnp.where(key_s[pl.ds(off + t * KEY_TILE, KEY_TILE), :] >= cand, 1, 0)
                parts.append(jnp.sum(ge.reshape(KEY_TILE // 8, 8, QB), axis=0))
            return acc + ((parts[0] + parts[1]) + (parts[2] + parts[3]))
        acc = lax.fori_loop(0, nst, body, jnp.zeros((8, QB), I32))
        return jnp.sum(acc, axis=0, keepdims=True)

    def bit_step(i, carry):
        prefix, cnt_ge = carry
        cand_u = prefix | lax.shift_left(jnp.int32(1), KEY_BITS - 1 - i)
        cnt = count_ge(cand_u + KEY_MIN)
        ok = cnt >= topk
        return jnp.where(ok, cand_u, prefix), jnp.where(ok, cnt, cnt_ge)

    prefix, cnt_ge = lax.fori_loop(
        0, KEY_BITS, bit_step,
        (jnp.zeros((1, QB), I32), jnp.zeros((1, QB), I32) + nst * KEY_SUPER))
    thresh = prefix + KEY_MIN
    excess = (cnt_ge - topk).astype(F32)

    def mask_tile(i, later_ties):
        off = pl.multiple_of((nst - 1 - i) * KEY_SUPER, KEY_SUPER)
        for t in reversed(range(TILES_PER_SUPER)):
            o = off + t * KEY_TILE
            key = key_s[pl.ds(o, KEY_TILE), :]
            eq = key == thresh
            eq_f = jnp.where(eq, 1.0, 0.0)
            suffix = later_ties + _dot(ut_ref[...], eq_f.astype(BF16))
            tie_bias = jnp.where(suffix > excess, 0.0, NEG_INF)
            bias = jnp.where(key > thresh, 0.0, jnp.where(eq, tie_bias, NEG_INF))
            bias_s[pl.ds(o, KEY_TILE), :] = jnp.where(o + row < limit, bias, NEG_INF)
            later_ties = later_ties + jnp.sum(eq_f, axis=0, keepdims=True)
        return later_ties

    lax.fori_loop(0, nst, mask_tile, jnp.zeros((1, QB), F32))

    acc_s[...] = jnp.zeros(acc_s.shape, F32)

    @pl.when(shift_is_safe)
    def _attend_shifted():
        def scores(st):
            off = pl.multiple_of(st * KEY_SUPER, KEY_SUPER)
            return _dot_nt(k1_s[pl.ds(off, KEY_SUPER), :], qs_s[...])

        def attend(st, sp):
            off = pl.multiple_of(st * KEY_SUPER, KEY_SUPER)
            bias = bias_s[pl.ds(off, KEY_SUPER), :]
            p = jnp.concatenate(
                [jnp.exp2(sp[:, h * QB:(h + 1) * QB] + bias).astype(BF16) for h in range(A_HEADS)],
                axis=1)
            acc_s[...] += _dot(vt_s[st], p)

        def attend_pair(i, _):
            sp0 = scores(2 * i)
            sp1 = scores(2 * i + 1)
            attend(2 * i, sp0)
            attend(2 * i + 1, sp1)
            return 0

        lax.fori_loop(0, nst >> 1, attend_pair, 0)

        @pl.when((nst & 1) == 1)
        def _odd_tail():
            attend(nst - 1, scores(nst - 1))

    @pl.when(jnp.logical_not(shift_is_safe))
    def _attend_online():
        m_s[...] = jnp.full(m_s.shape, NEG_INF, F32)

        def attend_tile(st, _):
            off = pl.multiple_of(st * KEY_SUPER, KEY_SUPER)
            k_tile = k1_s[pl.ds(off, KEY_SUPER), :]
            bias = bias_s[pl.ds(off, KEY_SUPER), :]
            for h in range(A_HEADS):
                cols = slice(h * QB, (h + 1) * QB)
                s = _dot_nt(k_tile, qs_s[cols, :]) + bias
                m_old = m_s[h:h + 1, :]
                m_new = jnp.maximum(m_old, jnp.max(s, axis=0, keepdims=True))
                m_safe = jnp.where(m_new == NEG_INF, 0.0, m_new)
                p = jnp.exp2(s - m_safe).astype(BF16)
                acc_s[:, cols] = jnp.exp2(m_old - m_safe) * acc_s[:, cols] + _dot(vt_s[st], p)
                m_s[h:h + 1, :] = m_new
            return 0

        lax.fori_loop(0, nst, attend_tile, 0)

    out_t = jnp.concatenate(
        [acc_s[0:A_HEAD_DIM, h * QB:(h + 1) * QB] / acc_s[A_HEAD_DIM:A_HEAD_DIM + 1, h * QB:(h + 1) * QB]
         for h in range(A_HEADS)], axis=0)
    o_ref[...] = out_t.T.astype(BF16)


def _dsa(q, iq, ikw, kv, qg, kg, bd, ut, B, T, QB):
    N = q.shape[0]
    nqb = T // QB
    topk = min(TOPK_MAX, T // 4)
    qrow = lambda b, j: (b * nqb + j, 0)
    brow = lambda b, j: (b, 0)
    return pl.pallas_call(
        functools.partial(_dsa_kernel, topk=topk, QB=QB),
        grid=(B, nqb),
        in_specs=[pl.BlockSpec((QB, 512), qrow), pl.BlockSpec((QB, 256), qrow),
                  pl.BlockSpec((QB, 128), qrow), pl.BlockSpec((T, 128), brow),
                  pl.BlockSpec((T, 128), brow), _resident((1, 512)), _resident((1, 128)),
                  _resident((512, 512)), _resident((KEY_TILE, KEY_TILE))],
        out_specs=pl.BlockSpec((QB, 512), qrow),
        out_shape=jax.ShapeDtypeStruct((N, 512), BF16),
        scratch_shapes=[
            pltpu.VMEM((T, 128), BF16),
            pltpu.VMEM((T, 128), BF16),
            pltpu.VMEM((T // KEY_SUPER, V_ROWS, KEY_SUPER), BF16),
            pltpu.VMEM((A_HEADS * QB, 128), BF16),
            pltpu.VMEM((IDX_HEADS * QB, 128), BF16),
            pltpu.VMEM((128, QB), F32),
            pltpu.VMEM((T, QB), I32),
            pltpu.VMEM((T, QB), F32),
            pltpu.VMEM((A_HEADS, QB), F32),
            pltpu.VMEM((V_ROWS, A_HEADS * QB), F32),
        ],
        compiler_params=_cparams(2),
        name="dsa_attention",
    )(q, iq, ikw, kv, ikw, qg, kg, bd, ut)


def _gelu_tanh(x):
    return 0.5 * x * (1.0 + jnp.tanh(math.sqrt(2.0 / math.pi) * (x + 0.044715 * (x * x * x))))


def _s5_kernel(u_ref, bm_ref, cm_ref, are_ref, aim_ref, d_ref, wg_ref, bg_ref, o_ref, buf_s, h_s, tm_s,
               *, B, steps):
    n_slab = bm_ref.shape[0]
    half = bm_ref.shape[2] // 2
    width = 2 * half
    n_lane_slabs = tm_s.shape[0]

    @pl.when(pl.program_id(0) == 0)
    def _init():
        h_s[...] = jnp.zeros(h_s.shape, F32)

    for b in range(B):
        for j in range(n_lane_slabs):
            c = b * 128 * n_lane_slabs + 128 * j
            tm_s[j, pl.ds(b, steps, stride=B), :] = u_ref[:, c:c + 128].astype(F32)
    u32 = jnp.concatenate([tm_s[j] for j in range(n_lane_slabs)], axis=1)
    u = u32.astype(BF16)

    def expand(k):
        buf_s[:, width * k:width * (k + 1)] = _dot(u[:, 128 * k:128 * (k + 1)], bm_ref[k])

    ys = []
    expand(0)
    for k in range(n_slab):
        if k + 1 < n_slab:
            expand(k + 1)
        c0 = width * k
        ar = jnp.broadcast_to(are_ref[:, half * k:half * (k + 1)], (B, half))
        ai = jnp.broadcast_to(aim_ref[:, half * k:half * (k + 1)], (B, half))
        hr, hi = h_s[:, c0:c0 + half], h_s[:, c0 + half:c0 + width]
        for t in range(steps):
            rows = slice(t * B, (t + 1) * B)
            hr, hi = (ar * hr - ai * hi + buf_s[rows, c0:c0 + half],
                      ar * hi + ai * hr + buf_s[rows, c0 + half:c0 + width])
            buf_s[rows, c0:c0 + half] = hr
            buf_s[rows, c0 + half:c0 + width] = hi
        h_s[:, c0:c0 + half] = hr
        h_s[:, c0 + half:c0 + width] = hi
        ys.append(_dot(buf_s[:, c0:c0 + width].astype(BF16), cm_ref[k]))
    y = jnp.concatenate(ys, axis=1)
    y = _gelu_tanh(y + d_ref[...] * u32)
    z = _dot(y.astype(BF16), wg_ref[...]) + bg_ref[...]
    out = y * (1.0 / (1.0 + jnp.exp(-z)))
    for j in range(n_lane_slabs):
        tm_s[j] = out[:, 128 * j:128 * (j + 1)]
    for b in range(B):
        for j in range(n_lane_slabs):
            c = b * 128 * n_lane_slabs + 128 * j
            o_ref[:, c:c + 128] = tm_s[j, pl.ds(b, steps, stride=B), :].astype(BF16)


def _s5(u_bt, bm, cm, a_re, a_im, d, wg, bg, B, T, steps):
    rows = steps * B
    width = u_bt.shape[1] // B
    n_state = bm.shape[0] * bm.shape[2]
    return pl.pallas_call(
        functools.partial(_s5_kernel, B=B, steps=steps),
        grid=(T // steps,),
        in_specs=[pl.BlockSpec((steps, B * width), lambda i: (i, 0)), _resident(bm.shape), _resident(cm.shape),
                  _resident(a_re.shape), _resident(a_im.shape), _resident(d.shape),
                  _resident(wg.shape), _resident(bg.shape)],
        out_specs=pl.BlockSpec((steps, B * width), lambda i: (i, 0)),
        out_shape=jax.ShapeDtypeStruct((T, B * width), BF16),
        scratch_shapes=[pltpu.VMEM((rows, n_state), F32),
                        pltpu.VMEM((B, n_state), F32),
                        pltpu.VMEM((width // 128, rows, 128), F32)],
        compiler_params=_cparams(1),
        name="s5_scan",
    )(u_bt, bm, cm, a_re, a_im, d, wg, bg)


def _reset_halo(halo_s, first_tile):
    @pl.when(first_tile)
    def _zero_halo():
        halo_s[0:CONV_HALO, :] = jnp.zeros((CONV_HALO, halo_s.shape[1]), F32)


def _causal_conv3(c, halo_s, w_ref, b_ref, cols):
    tm = c.shape[0]
    halo_s[CONV_HALO:CONV_HALO + tm, cols] = c
    c1 = halo_s[CONV_HALO - 1:CONV_HALO - 1 + tm, cols]
    c2 = halo_s[CONV_HALO - 2:CONV_HALO - 2 + tm, cols]
    y = w_ref[0:1, cols] * c2 + w_ref[1:2, cols] * c1 + w_ref[2:3, cols] * c + b_ref[:, cols]
    halo_s[0:CONV_HALO, cols] = halo_s[tm:tm + CONV_HALO, cols]
    return y


def _odd_kernel(x_ref, g_ref, wi_ref, cw_ref, cb_ref, wo_ref, o_ref, halo_s):
    D = x_ref.shape[1]
    x = x_ref[...]
    hn = _rms_rows(x, g_ref[...]).astype(BF16)
    _reset_halo(halo_s, pl.program_id(1) == 0)
    acc = x
    starts = list(range(0, D, ODD_CHUNK))

    def in_proj(c0):
        return tuple(_dot(hn, wi_ref[:, part * D + c0:part * D + c0 + ODD_CHUNK]) for part in range(3))

    nxt = in_proj(0)
    for n, c0 in enumerate(starts):
        gb, gc, z = nxt
        if n + 1 < len(starts):
            nxt = in_proj(starts[n + 1])
        cols = slice(c0, c0 + ODD_CHUNK)
        conv = _causal_conv3(gc * z, halo_s, cw_ref, cb_ref, cols)
        acc = acc + _dot((gb * conv).astype(BF16), wo_ref[cols, :])
    o_ref[...] = acc


def _odd_mixer(x2, g, wi, cw, cb, wo, B, T, tm):
    N, D = x2.shape
    nt = T // tm
    row = lambda b, t: (b * nt + t, 0)
    return pl.pallas_call(
        _odd_kernel,
        grid=(B, nt),
        in_specs=[pl.BlockSpec((tm, D), row), _resident((1, D)), _resident(wi.shape),
                  _resident(cw.shape), _resident(cb.shape), _resident(wo.shape)],
        out_specs=pl.BlockSpec((tm, D), row),
        out_shape=jax.ShapeDtypeStruct((N, D), F32),
        scratch_shapes=[pltpu.VMEM((tm + CONV_HALO, D), F32)],
        compiler_params=_cparams(2),
        name="odd_mixer",
    )(x2, g, wi, cw, cb, wo)


def _mem_kv_kernel(m_ref, g_ref, w_ref, kg_ref, k_ref, v_ref):
    hn = _rms_rows(m_ref[...], g_ref[0])
    y = _dot(hn.astype(BF16), w_ref[0])
    ks = []
    for h in range(X_HEADS):
        ks.append(_rms_rows(y[:, X_HEAD_DIM * h:X_HEAD_DIM * (h + 1)], kg_ref[0]))
    k_ref[0] = jnp.concatenate(ks, axis=1).astype(BF16)
    v_ref[0] = y[:, X_WIDTH:2 * X_WIDTH].astype(BF16)


def _mem_kv(mem2, g, w, kg):
    depth = w.shape[0]
    NM, D = mem2.shape
    return pl.pallas_call(
        _mem_kv_kernel,
        grid=(depth,),
        in_specs=[_resident((NM, D)), pl.BlockSpec((1, 1, D), lambda i: (i, 0, 0)),
                  pl.BlockSpec((1, D, 2 * X_WIDTH), lambda i: (i, 0, 0)),
                  pl.BlockSpec((1, 1, X_HEAD_DIM), lambda i: (i, 0, 0))],
        out_specs=[pl.BlockSpec((1, NM, X_WIDTH), lambda i: (i, 0, 0)),
                   pl.BlockSpec((1, NM, X_WIDTH), lambda i: (i, 0, 0))],
        out_shape=[jax.ShapeDtypeStruct((depth, NM, X_WIDTH), BF16)] * 2,
        compiler_params=_cparams(1),
        name="mem_kv",
    )(mem2, g, w, kg)


def _xattn_kernel(*refs, mixer_out):
    if mixer_out:
        x_ref, ya_ref, yb_ref, wm_ref, g_ref, wq_ref, qg_ref, k_ref, v_ref, wo_ref, o_ref = refs
        na = ya_ref.shape[1]
        x = x_ref[...] + _dot(ya_ref[...], wm_ref[0:na, :]) + _dot(yb_ref[...], wm_ref[na:, :])
    else:
        x_ref, g_ref, wq_ref, qg_ref, k_ref, v_ref, wo_ref, o_ref = refs
        x = x_ref[...]
    hn = _rms_rows(x, g_ref[...])
    q = _dot(hn.astype(BF16), wq_ref[...])
    k = k_ref[0]
    v = v_ref[0]
    heads = [slice(X_HEAD_DIM * h, X_HEAD_DIM * (h + 1)) for h in range(X_HEADS)]
    scores = [_dot_nt((_rms_rows(q[:, sl], qg_ref[...]) * (X_HEAD_DIM ** -0.5)).astype(BF16), k[:, sl])
              for sl in heads]
    outs = []
    for sl, s in zip(heads, scores):
        p = jnp.exp(s - jnp.max(s, axis=-1, keepdims=True))
        outs.append(_dot(p.astype(BF16), v[:, sl]) / jnp.sum(p, axis=-1, keepdims=True))
    o_ref[...] = x + _dot(jnp.concatenate(outs, axis=1).astype(BF16), wo_ref[...])


def _xattn(x2, mixer_out, g, wq, qg, k_all, v_all, wo, layer, B, T, tm, M):
    N, D = x2.shape
    nt = T // tm
    row = lambda b, t: (b * nt + t, 0)
    mrow = lambda b, t: (layer * B + b, 0, 0)
    mixer_args, mixer_specs = (), []
    if mixer_out is not None:
        ya, yb_t, w_mix = mixer_out
        mixer_args = (ya, yb_t, w_mix)
        mixer_specs = [pl.BlockSpec((tm, ya.shape[1]), row),
                       pl.BlockSpec((tm, w_mix.shape[0] - ya.shape[1]), lambda b, t: (t, b)), _resident(w_mix.shape)]
    return pl.pallas_call(
        functools.partial(_xattn_kernel, mixer_out=mixer_out is not None),
        grid=(B, nt),
        in_specs=[pl.BlockSpec((tm, D), row)] + mixer_specs + [
            _resident((1, D)), _resident(wq.shape), _resident((1, X_HEAD_DIM)),
            pl.BlockSpec((1, M, X_WIDTH), mrow), pl.BlockSpec((1, M, X_WIDTH), mrow), _resident(wo.shape)],
        out_specs=pl.BlockSpec((tm, D), row),
        out_shape=jax.ShapeDtypeStruct((N, D), F32),
        compiler_params=_cparams(2),
        name="mem_xattn",
    )(x2, *mixer_args, g, wq, qg, k_all, v_all, wo)


def _ffn_kernel(x_ref, g_ref, wu_ref, cw_ref, cb_ref, wd_ref, o_ref, halo_s):
    F = cw_ref.shape[1]
    x = x_ref[...]
    hn = _rms_rows(x, g_ref[...]).astype(BF16)
    _reset_halo(halo_s, pl.program_id(1) == 0)
    acc = x
    starts = list(range(0, F, MLP_CHUNK))

    def up_proj(c0):
        return _dot(hn, wu_ref[:, c0:c0 + MLP_CHUNK]), _dot(hn, wu_ref[:, F + c0:F + c0 + MLP_CHUNK])

    nxt = up_proj(0)
    for n, c0 in enumerate(starts):
        gate, up = nxt
        if n + 1 < len(starts):
            nxt = up_proj(starts[n + 1])
        cols = slice(c0, c0 + MLP_CHUNK)
        conv = _causal_conv3(gate, halo_s, cw_ref, cb_ref, cols)
        act = conv * (1.0 / (1.0 + jnp.exp(-conv))) * up
        acc = acc + _dot(act.astype(BF16), wd_ref[cols, :])
    o_ref[...] = acc


def _ffn(x2, g, wu, cw, cb, wd, B, T, tm):
    N, D = x2.shape
    F = cw.shape[1]
    nt = T // tm
    row = lambda b, t: (b * nt + t, 0)
    return pl.pallas_call(
        _ffn_kernel,
        grid=(B, nt),
        in_specs=[pl.BlockSpec((tm, D), row), _resident((1, D)),
                  pl.BlockSpec(wu.shape, lambda b, t: (0, 0), pipeline_mode=pl.Buffered(1)),
                  _resident(cw.shape), _resident(cb.shape),
                  pl.BlockSpec(wd.shape, lambda b, t: (0, 0), pipeline_mode=pl.Buffered(1))],
        out_specs=pl.BlockSpec((tm, D), row),
        out_shape=jax.ShapeDtypeStruct((N, D), F32),
        scratch_shapes=[pltpu.VMEM((tm + CONV_HALO, F), F32)],
        compiler_params=_cparams(2),
        name="conv_glu_ffn",
    )(x2, g, wu, cw, cb, wd)


def _s5_operators(lam_re, lam_im, log_step, b_re, b_im, c_re, c_im):
    G, P = lam_re.shape
    C = b_re.shape[-1]
    S = S5_GROUPS_PER_SLAB
    n_slab = G // S
    delta = jnp.exp(log_step)[:, None]
    mag = jnp.exp(lam_re * delta)
    bar_re = mag * jnp.cos(lam_im * delta)
    bar_im = mag * jnp.sin(lam_im * delta)
    den = lam_re * lam_re + lam_im * lam_im
    coef_re = ((bar_re - 1.0) * lam_re + bar_im * lam_im) / den
    coef_im = (bar_im * lam_re - (bar_re - 1.0) * lam_im) / den
    bb_re = coef_re[..., None] * b_re - coef_im[..., None] * b_im
    bb_im = coef_re[..., None] * b_im + coef_im[..., None] * b_re
    eye = jnp.eye(S, dtype=F32)

    def in_blocks(m):
        return jnp.einsum('kgpc,gh->kgchp', m.reshape(n_slab, S, P, C), eye).reshape(n_slab, S * C, S * P)

    def out_blocks(m):
        return jnp.einsum('kgcp,gh->kgphc', m.reshape(n_slab, S, C, P), eye).reshape(n_slab, S * P, S * C)

    bm = jnp.concatenate([in_blocks(bb_re), in_blocks(bb_im)], axis=2)
    cm = jnp.concatenate([out_blocks(c_re), -out_blocks(c_im)], axis=1)
    a_re = bar_re.reshape(1, G * P)
    a_im = bar_im.reshape(1, G * P)
    return bm.astype(BF16), cm.astype(BF16), a_re, a_im


def _even_w_in_layout(w):
    n_front = A_WIDTH + 2 * A_HEAD_DIM + IDX_HEADS * IDX_DIM + IDX_DIM + IDX_HEADS
    pad = jnp.zeros((w.shape[0], 128 - IDX_DIM - IDX_HEADS), w.dtype)
    return jnp.concatenate([w[:, :n_front], pad, w[:, n_front:]], axis=1).astype(BF16)


def kernel(x, mem, norm_mix, norm_x, norm_mem, norm_ffn, even_w_in, even_w_out, a_q_norm, a_k_norm,
           s5_lam_re, s5_lam_im, s5_log_step, s5_b_re, s5_b_im, s5_c_re, s5_c_im, s5_d, s5_w_glu, s5_b_glu,
           odd_w_in, odd_conv_w, odd_conv_b, odd_w_out, x_w_q, x_w_kv, x_w_o, x_q_norm, x_k_norm,
           f_w_up, f_conv_w, f_conv_b, f_w_down):
    B, T, D = x.shape
    M = mem.shape[1]
    depth = norm_mix.shape[0]
    assert D == 1024 and T % KEY_SUPER == 0 and even_w_in.shape[2] == 1476
    tm_proj, tm_fused, s5_steps = 512, 256, 64

    x2 = x.reshape(B * T, D)
    mem2 = mem.reshape(B * M, D)

    k_all, v_all = _mem_kv(mem2, norm_mem.reshape(depth, 1, D), x_w_kv.astype(BF16),
                           x_k_norm.reshape(depth, 1, X_HEAD_DIM))
    k_all = k_all.reshape(depth * B, M, X_WIDTH)
    v_all = v_all.reshape(depth * B, M, X_WIDTH)

    head_mean = jnp.kron(jnp.eye(A_HEADS, dtype=F32),
                         jnp.full((A_HEAD_DIM, A_HEAD_DIM), 1.0 / A_HEAD_DIM, F32)).astype(BF16)
    upper_tri = jnp.triu(jnp.ones((KEY_TILE, KEY_TILE), F32)).astype(BF16)

    for i in range(depth):
        j = i // 2
        g_mix = norm_mix[i].reshape(1, D)
        if i % 2 == 0:
            q, kv, iq, ikw, u_t = _even_in(x2, g_mix, _even_w_in_layout(even_w_in[j]), B, T, tm_proj)
            qg = jnp.tile(a_q_norm[j], A_HEADS).reshape(1, A_WIDTH)
            kg = jnp.concatenate([a_k_norm[j], jnp.zeros((128 - A_HEAD_DIM,), F32)]).reshape(1, 128)
            ya = _dsa(q, iq, ikw, kv, qg, kg, head_mean, upper_tri, B, T, Q_BLOCK)
            bm, cm, a_re, a_im = _s5_operators(s5_lam_re[j], s5_lam_im[j], s5_log_step[j],
                                               s5_b_re[j], s5_b_im[j], s5_c_re[j], s5_c_im[j])
            yb_t = _s5(u_t, bm, cm, a_re, a_im, s5_d[j].reshape(1, 512),
                        s5_w_glu[j].astype(BF16), s5_b_glu[j].reshape(1, 512), B, T, s5_steps)
            mixer_out = (ya, yb_t, even_w_out[j].astype(BF16))
        else:
            x2 = _odd_mixer(x2, g_mix, odd_w_in[j].astype(BF16), odd_conv_w[j],
                            odd_conv_b[j].reshape(1, D), odd_w_out[j].astype(BF16), B, T, tm_fused)
            mixer_out = None
        x2 = _xattn(x2, mixer_out, norm_x[i].reshape(1, D), x_w_q[i].astype(BF16),
                    x_q_norm[i].reshape(1, X_HEAD_DIM), k_all, v_all, x_w_o[i].astype(BF16), i, B, T, tm_proj, M)
        x2 = _ffn(x2, norm_ffn[i].reshape(1, D), f_w_up[i].astype(BF16), f_conv_w[i],
                  f_conv_b[i].reshape(1, -1), f_w_down[i].astype(BF16), B, T, tm_fused)
    return x2.reshape(B, T, D)
```

```python
import functools
import math

import jax
import jax.numpy as jnp
from jax import lax
from jax.experimental import pallas as pl
from jax.experimental.pallas import tpu as pltpu

F32 = jnp.float32
BF16 = jnp.bfloat16
I32 = jnp.int32

EPS = 1e-6
CHUNK = 64
Q_BLOCK = 512
KEY_TILE = 128
TILES_PER_SUPER = 4
KEY_SUPER = KEY_TILE * TILES_PER_SUPER
KEY_BITS = 16
KEY_MIN = -(1 << (KEY_BITS - 1))
V_ROWS = 80
LOG2_E = 1.4426950408889634
SHIFT_SLACK = 1.02
MAX_SAFE_SHIFT = 50.0
A_HEADS = 8
A_HEAD_DIM = 64
A_WIDTH = A_HEADS * A_HEAD_DIM
IDX_HEADS = 4
IDX_DIM = 64
TOPK_MAX = 256
S5_GROUPS_PER_SLAB = 8
X_HEADS = 4
X_HEAD_DIM = 128
X_WIDTH = X_HEADS * X_HEAD_DIM
CONV_HALO = 8
MLP_CHUNK = 256
ODD_CHUNK = 256

V7X_VMEM_LIMIT_BYTES = 56 * 1024 * 1024
NEG_INF = float("-inf")


def _cparams(n_axes):
    return pltpu.CompilerParams(
        dimension_semantics=("arbitrary",) * n_axes,
        vmem_limit_bytes=V7X_VMEM_LIMIT_BYTES)


def _rms_rows(xf, g):
    ms = jnp.mean(xf * xf, axis=-1, keepdims=True)
    return xf * lax.rsqrt(ms + EPS) * g


def _dot(a, b):
    return jnp.dot(a, b, preferred_element_type=F32)


def _dot_nt(a, b):
    return lax.dot_general(a, b, (((1,), (1,)), ((), ())), preferred_element_type=F32)


def _resident(shape):
    nd = len(shape)
    return pl.BlockSpec(shape, lambda *_: (0,) * nd)


def _even_in_kernel(x_ref, g_ref, w_ref, q_ref, kv_ref, iq_ref, ikw_ref, u_ref):
    hn = _rms_rows(x_ref[...], g_ref[...])
    y = _dot(hn.astype(BF16), w_ref[...])
    q_ref[...] = y[:, 0:512].astype(BF16)
    kv_ref[...] = y[:, 512:640].astype(BF16)
    iq_ref[...] = y[:, 640:896].astype(BF16)
    ikw_ref[...] = y[:, 896:1024]
    u_ref[...] = y[:, 1024:1536].astype(BF16)


def _even_in(x2, g, w, B, T, tm):
    N, D = x2.shape
    nt = T // tm
    row = lambda b, t: (b * nt + t, 0)
    return pl.pallas_call(
        _even_in_kernel,
        grid=(B, nt),
        in_specs=[pl.BlockSpec((tm, D), row), _resident((1, D)), _resident(w.shape)],
        out_specs=[pl.BlockSpec((tm, 512), row), pl.BlockSpec((tm, 128), row),
                   pl.BlockSpec((tm, 256), row), pl.BlockSpec((tm, 128), row),
                   pl.BlockSpec((tm, 512), lambda b, t: (t, b))],
        out_shape=[jax.ShapeDtypeStruct((N, 512), BF16), jax.ShapeDtypeStruct((N, 128), BF16),
                   jax.ShapeDtypeStruct((N, 256), BF16), jax.ShapeDtypeStruct((N, 128), F32),
                   jax.ShapeDtypeStruct((T, B * 512), BF16)],
        compiler_params=_cparams(2),
        name="even_in",
    )(x2, g, w)


def _dsa_kernel(q_ref, iq_ref, iwq_ref, kv_ref, ikf_ref, qg_ref, kg_ref, bd_ref, ut_ref, o_ref,
                k1_s, ik2_s, vt_s, qs_s, iqs_s, wt_s, key_s, bias_s, m_s, acc_s, *, topk, QB):
    T = kv_ref.shape[0]
    qb = pl.program_id(1)
    nst = ((qb + 1) * (QB // KEY_TILE) + TILES_PER_SUPER - 1) >> (TILES_PER_SUPER.bit_length() - 1)

    qlane = lax.broadcasted_iota(I32, (KEY_TILE, QB), 1)
    row = lax.broadcasted_iota(I32, (KEY_TILE, QB), 0)
    lane = lax.broadcasted_iota(I32, (KEY_TILE, 128), 1)
    lo_half = lane < A_HEAD_DIM
    shift_lane = lane == A_HEAD_DIM

    q_scale = (A_HEAD_DIM ** -0.5) * LOG2_E
    shift = (A_HEAD_DIM * q_scale * SHIFT_SLACK) * jnp.max(jnp.abs(qg_ref[...])) * jnp.max(jnp.abs(kg_ref[...]))
    shift_is_safe = shift <= MAX_SAFE_SHIFT

    @pl.when(qb == 0)
    def _prepare_batch():
        ones_rows = jnp.where(lax.broadcasted_iota(I32, (V_ROWS - A_HEAD_DIM, KEY_TILE), 0) == 0, 1.0, 0.0)

        def key_tile(st, _):
            for t in range(TILES_PER_SUPER):
                off = pl.multiple_of(st * KEY_SUPER + t * KEY_TILE, KEY_TILE)
                kv = kv_ref[pl.ds(off, KEY_TILE), :].astype(F32)
                k = jnp.where(lo_half, kv, 0.0)
                ms = jnp.sum(k * k, axis=-1, keepdims=True) * (1.0 / A_HEAD_DIM)
                kn = k * lax.rsqrt(ms + EPS) * kg_ref[...]
                k1_s[pl.ds(off, KEY_TILE), :] = jnp.where(shift_lane, 1.0, kn).astype(BF16)
                ik = jnp.where(lo_half, ikf_ref[pl.ds(off, KEY_TILE), :], 0.0)
                ik2_s[pl.ds(off, KEY_TILE), :] = (ik + pltpu.roll(ik, 64, 1)).astype(BF16)
                v = pltpu.roll(jnp.where(lo_half, 0.0, kv), 64, 1)
                vt1 = jnp.concatenate([v.T[0:A_HEAD_DIM, :], ones_rows], axis=0)
                vt_s[st, :, t * KEY_TILE:(t + 1) * KEY_TILE] = vt1.astype(BF16)
            return 0

        lax.fori_loop(0, T // KEY_SUPER, key_tile, 0)

    q = q_ref[...].astype(F32)
    ms = _dot((q * q).astype(BF16), bd_ref[...])
    qn = q * lax.rsqrt(ms + EPS) * qg_ref[...] * q_scale
    half_q = lax.broadcasted_iota(I32, (QB, 128), 1) < A_HEAD_DIM
    neg_shift_lane = jnp.where(lax.broadcasted_iota(I32, (QB, 128), 1) == A_HEAD_DIM, -1.0, 0.0) * shift
    for h in range(A_HEADS):
        blk = qn[:, 128 * (h // 2):128 * (h // 2) + 128]
        if h % 2 == 1:
            blk = pltpu.roll(blk, 64, 1)
        qs_s[h * QB:(h + 1) * QB, :] = (jnp.where(half_q, blk, 0.0) + neg_shift_lane).astype(BF16)
    iq = iq_ref[...].astype(F32)
    for h in range(IDX_HEADS):
        blk = iq[:, 128 * (h // 2):128 * (h // 2) + 128]
        keep = half_q if h % 2 == 0 else jnp.logical_not(half_q)
        iqs_s[h * QB:(h + 1) * QB, :] = jnp.where(keep, blk, 0.0).astype(BF16)
    wt_s[...] = iwq_ref[...].T * ((IDX_DIM ** -0.5) * (IDX_HEADS ** -0.5))

    chunk_shift = CHUNK.bit_length() - 1
    limit = (((qb * QB + qlane) >> chunk_shift) + 1) << chunk_shift

    def score_tile(st, masked):
        off = pl.multiple_of(st * KEY_SUPER, KEY_SUPER)
        x = _dot_nt(ik2_s[pl.ds(off, KEY_SUPER), :], iqs_s[...])
        for t in range(TILES_PER_SUPER):
            rows = slice(t * KEY_TILE, (t + 1) * KEY_TILE)
            sc = wt_s[64:65, :] * jnp.maximum(x[rows, 0:QB], 0.0)
            for h in range(1, IDX_HEADS):
                sc = sc + wt_s[64 + h:65 + h, :] * jnp.maximum(x[rows, QB * h:QB * (h + 1)], 0.0)
            sc = jnp.where(sc == 0.0, 0.0, sc)
            if masked:
                sc = jnp.where(off + t * KEY_TILE + row < limit, sc, NEG_INF)
            bits = pltpu.bitcast(sc.astype(BF16).astype(F32), I32) >> 16
            key_s[pl.ds(off + t * KEY_TILE, KEY_TILE), :] = bits ^ ((bits >> 31) & 0x7FFF)

    def score_full_tile(st, carry):
        score_tile(st, False)
        return carry

    lax.fori_loop(0, nst - 1, score_full_tile, 0)
    score_tile(nst - 1, True)

    def count_ge(cand):
        def body(st, acc):
            off = pl.multiple_of(st * KEY_SUPER, KEY_SUPER)
            parts = []
            for t in range(TILES_PER_SUPER):
                ge = jnp.where(key_s[pl.ds(off + t * KEY_TILE, KEY_TILE), :] >= cand, 1, 0)
                parts.append(jnp.sum(ge.reshape(KEY_TILE // 8, 8, QB), axis=0))
            return acc + ((parts[0] + parts[1]) + (parts[2] + parts[3]))
        acc = lax.fori_loop(0, nst, body, jnp.zeros((8, QB), I32))
        return jnp.sum(acc, axis=0, keepdims=True)

    def bit_step(i, carry):
        prefix, cnt_ge = carry
        cand_u = prefix | lax.shift_left(jnp.int32(1), KEY_BITS - 1 - i)
        cnt = count_ge(cand_u + KEY_MIN)
        ok = cnt >= topk
        return jnp.where(ok, cand_u, prefix), jnp.where(ok, cnt, cnt_ge)

    prefix, cnt_ge = lax.fori_loop(
        0, KEY_BITS, bit_step,
        (jnp.zeros((1, QB), I32), jnp.zeros((1, QB), I32) + nst * KEY_SUPER))
    thresh = prefix + KEY_MIN
    excess = (cnt_ge - topk).astype(F32)

    def mask_tile(st, later_ties, masked):
        off = pl.multiple_of(st * KEY_SUPER, KEY_SUPER)
        for t in reversed(range(TILES_PER_SUPER)):
            o = off + t * KEY_TILE
            key = key_s[pl.ds(o, KEY_TILE), :]
            eq = key == thresh
            eq_f = jnp.where(eq, 1.0, 0.0)
            suffix = later_ties + _dot(ut_ref[...], eq_f.astype(BF16))
            tie_bias = jnp.where(suffix > excess, 0.0, NEG_INF)
            bias = jnp.where(key > thresh, 0.0, jnp.where(eq, tie_bias, NEG_INF))
            if masked:
                bias = jnp.where(o + row < limit, bias, NEG_INF)
            bias_s[pl.ds(o, KEY_TILE), :] = bias
            later_ties = later_ties + jnp.sum(eq_f, axis=0, keepdims=True)
        return later_ties

    ties_in_last = mask_tile(nst - 1, jnp.zeros((1, QB), F32), True)
    lax.fori_loop(0, nst - 1, lambda i, ties: mask_tile(nst - 2 - i, ties, False), ties_in_last)

    acc_s[...] = jnp.zeros(acc_s.shape, F32)

    @pl.when(shift_is_safe)
    def _attend_shifted():
        def scores(st):
            off = pl.multiple_of(st * KEY_SUPER, KEY_SUPER)
            return _dot_nt(k1_s[pl.ds(off, KEY_SUPER), :], qs_s[...])

        def attend(st, sp):
            off = pl.multiple_of(st * KEY_SUPER, KEY_SUPER)
            bias = bias_s[pl.ds(off, KEY_SUPER), :]
            p = jnp.concatenate(
                [jnp.exp2(sp[:, h * QB:(h + 1) * QB] + bias).astype(BF16) for h in range(A_HEADS)],
                axis=1)
            acc_s[...] += _dot(vt_s[st], p)

        def attend_pair(i, _):
            sp0 = scores(2 * i)
            sp1 = scores(2 * i + 1)
            attend(2 * i, sp0)
            attend(2 * i + 1, sp1)
            return 0

        lax.fori_loop(0, nst >> 1, attend_pair, 0)

        @pl.when((nst & 1) == 1)
        def _odd_tail():
            attend(nst - 1, scores(nst - 1))

    @pl.when(jnp.logical_not(shift_is_safe))
    def _attend_online():
        m_s[...] = jnp.full(m_s.shape, NEG_INF, F32)

        def attend_tile(st, _):
            off = pl.multiple_of(st * KEY_SUPER, KEY_SUPER)
            k_tile = k1_s[pl.ds(off, KEY_SUPER), :]
            bias = bias_s[pl.ds(off, KEY_SUPER), :]
            for h in range(A_HEADS):
                cols = slice(h * QB, (h + 1) * QB)
                s = _dot_nt(k_tile, qs_s[cols, :]) + bias
                m_old = m_s[h:h + 1, :]
                m_new = jnp.maximum(m_old, jnp.max(s, axis=0, keepdims=True))
                m_safe = jnp.where(m_new == NEG_INF, 0.0, m_new)
                p = jnp.exp2(s - m_safe).astype(BF16)
                acc_s[:, cols] = jnp.exp2(m_old - m_safe) * acc_s[:, cols] + _dot(vt_s[st], p)
                m_s[h:h + 1, :] = m_new
            return 0

        lax.fori_loop(0, nst, attend_tile, 0)

    out_t = jnp.concatenate(
        [acc_s[0:A_HEAD_DIM, h * QB:(h + 1) * QB] / acc_s[A_HEAD_DIM:A_HEAD_DIM + 1, h * QB:(h + 1) * QB]
         for h in range(A_HEADS)], axis=0)
    o_ref[...] = out_t.T.astype(BF16)


def _dsa(q, iq, ikw, kv, qg, kg, bd, ut, B, T, QB):
    N = q.shape[0]
    nqb = T // QB
    topk = min(TOPK_MAX, T // 4)
    qrow = lambda b, j: (b * nqb + j, 0)
    brow = lambda b, j: (b, 0)
    return pl.pallas_call(
        functools.partial(_dsa_kernel, topk=topk, QB=QB),
        grid=(B, nqb),
        in_specs=[pl.BlockSpec((QB, 512), qrow), pl.BlockSpec((QB, 256), qrow),
                  pl.BlockSpec((QB, 128), qrow), pl.BlockSpec((T, 128), brow),
                  pl.BlockSpec((T, 128), brow), _resident((1, 512)), _resident((1, 128)),
                  _resident((512, 512)), _resident((KEY_TILE, KEY_TILE))],
        out_specs=pl.BlockSpec((QB, 512), qrow),
        out_shape=jax.ShapeDtypeStruct((N, 512), BF16),
        scratch_shapes=[
            pltpu.VMEM((T, 128), BF16),
            pltpu.VMEM((T, 128), BF16),
            pltpu.VMEM((T // KEY_SUPER, V_ROWS, KEY_SUPER), BF16),
            pltpu.VMEM((A_HEADS * QB, 128), BF16),
            pltpu.VMEM((IDX_HEADS * QB, 128), BF16),
            pltpu.VMEM((128, QB), F32),
            pltpu.VMEM((T, QB), I32),
            pltpu.VMEM((T, QB), F32),
            pltpu.VMEM((A_HEADS, QB), F32),
            pltpu.VMEM((V_ROWS, A_HEADS * QB), F32),
        ],
        compiler_params=_cparams(2),
        name="dsa_attention",
    )(q, iq, ikw, kv, ikw, qg, kg, bd, ut)


def _gelu_tanh(x):
    return 0.5 * x * (1.0 + jnp.tanh(math.sqrt(2.0 / math.pi) * (x + 0.044715 * (x * x * x))))


def _s5_kernel(u_ref, bm_ref, cm_ref, are_ref, aim_ref, d_ref, wg_ref, bg_ref, o_ref, buf_s, h_s, tm_s,
               *, B, steps):
    n_slab = bm_ref.shape[0]
    half = bm_ref.shape[2] // 2
    width = 2 * half
    n_lane_slabs = tm_s.shape[0]

    @pl.when(pl.program_id(0) == 0)
    def _init():
        h_s[...] = jnp.zeros(h_s.shape, F32)

    for b in range(B):
        for j in range(n_lane_slabs):
            c = b * 128 * n_lane_slabs + 128 * j
            tm_s[j, pl.ds(b, steps, stride=B), :] = u_ref[:, c:c + 128].astype(F32)
    u32 = jnp.concatenate([tm_s[j] for j in range(n_lane_slabs)], axis=1)
    u = u32.astype(BF16)

    def expand(k):
        buf_s[:, width * k:width * (k + 1)] = _dot(u[:, 128 * k:128 * (k + 1)], bm_ref[k])

    ys = []
    expand(0)
    for k in range(n_slab):
        if k + 1 < n_slab:
            expand(k + 1)
        c0 = width * k
        ar = jnp.broadcast_to(are_ref[:, half * k:half * (k + 1)], (B, half))
        ai = jnp.broadcast_to(aim_ref[:, half * k:half * (k + 1)], (B, half))
        hr, hi = h_s[:, c0:c0 + half], h_s[:, c0 + half:c0 + width]
        for t in range(steps):
            rows = slice(t * B, (t + 1) * B)
            hr, hi = (ar * hr - ai * hi + buf_s[rows, c0:c0 + half],
                      ar * hi + ai * hr + buf_s[rows, c0 + half:c0 + width])
            buf_s[rows, c0:c0 + half] = hr
            buf_s[rows, c0 + half:c0 + width] = hi
        h_s[:, c0:c0 + half] = hr
        h_s[:, c0 + half:c0 + width] = hi
        ys.append(_dot(buf_s[:, c0:c0 + width].astype(BF16), cm_ref[k]))
    y = jnp.concatenate(ys, axis=1)
    y = _gelu_tanh(y + d_ref[...] * u32)
    z = _dot(y.astype(BF16), wg_ref[...]) + bg_ref[...]
    out = y * (1.0 / (1.0 + jnp.exp(-z)))
    for j in range(n_lane_slabs):
        tm_s[j] = out[:, 128 * j:128 * (j + 1)]
    for b in range(B):
        for j in range(n_lane_slabs):
            c = b * 128 * n_lane_slabs + 128 * j
            o_ref[:, c:c + 128] = tm_s[j, pl.ds(b, steps, stride=B), :].astype(BF16)


def _s5(u_bt, bm, cm, a_re, a_im, d, wg, bg, B, T, steps):
    rows = steps * B
    width = u_bt.shape[1] // B
    n_state = bm.shape[0] * bm.shape[2]
    return pl.pallas_call(
        functools.partial(_s5_kernel, B=B, steps=steps),
        grid=(T // steps,),
        in_specs=[pl.BlockSpec((steps, B * width), lambda i: (i, 0)), _resident(bm.shape), _resident(cm.shape),
                  _resident(a_re.shape), _resident(a_im.shape), _resident(d.shape),
                  _resident(wg.shape), _resident(bg.shape)],
        out_specs=pl.BlockSpec((steps, B * width), lambda i: (i, 0)),
        out_shape=jax.ShapeDtypeStruct((T, B * width), BF16),
        scratch_shapes=[pltpu.VMEM((rows, n_state), F32),
                        pltpu.VMEM((B, n_state), F32),
                        pltpu.VMEM((width // 128, rows, 128), F32)],
        compiler_params=_cparams(1),
        name="s5_scan",
    )(u_bt, bm, cm, a_re, a_im, d, wg, bg)


def _reset_halo(halo_s, first_tile):
    @pl.when(first_tile)
    def _zero_halo():
        halo_s[0:CONV_HALO, :] = jnp.zeros((CONV_HALO, halo_s.shape[1]), F32)


def _causal_conv3(c, halo_s, w_ref, b_ref, cols):
    tm = c.shape[0]
    halo_s[CONV_HALO:CONV_HALO + tm, cols] = c
    c1 = halo_s[CONV_HALO - 1:CONV_HALO - 1 + tm, cols]
    c2 = halo_s[CONV_HALO - 2:CONV_HALO - 2 + tm, cols]
    y = w_ref[0:1, cols] * c2 + w_ref[1:2, cols] * c1 + w_ref[2:3, cols] * c + b_ref[:, cols]
    halo_s[0:CONV_HALO, cols] = halo_s[tm:tm + CONV_HALO, cols]
    return y


def _odd_kernel(x_ref, g_ref, wi_ref, cw_ref, cb_ref, wo_ref, o_ref, halo_s):
    D = x_ref.shape[1]
    x = x_ref[...]
    hn = _rms_rows(x, g_ref[...]).astype(BF16)
    _reset_halo(halo_s, pl.program_id(1) == 0)
    acc = x
    starts = list(range(0, D, ODD_CHUNK))

    def in_proj(c0):
        return tuple(_dot(hn, wi_ref[:, part * D + c0:part * D + c0 + ODD_CHUNK]) for part in range(3))

    nxt = in_proj(0)
    for n, c0 in enumerate(starts):
        gb, gc, z = nxt
        if n + 1 < len(starts):
            nxt = in_proj(starts[n + 1])
        cols = slice(c0, c0 + ODD_CHUNK)
        conv = _causal_conv3(gc * z, halo_s, cw_ref, cb_ref, cols)
        acc = acc + _dot((gb * conv).astype(BF16), wo_ref[cols, :])
    o_ref[...] = acc


def _odd_mixer(x2, g, wi, cw, cb, wo, B, T, tm):
    N, D = x2.shape
    nt = T // tm
    row = lambda b, t: (b * nt + t, 0)
    return pl.pallas_call(
        _odd_kernel,
        grid=(B, nt),
        in_specs=[pl.BlockSpec((tm, D), row), _resident((1, D)), _resident(wi.shape),
                  _resident(cw.shape), _resident(cb.shape), _resident(wo.shape)],
        out_specs=pl.BlockSpec((tm, D), row),
        out_shape=jax.ShapeDtypeStruct((N, D), F32),
        scratch_shapes=[pltpu.VMEM((tm + CONV_HALO, D), F32)],
        compiler_params=_cparams(2),
        name="odd_mixer",
    )(x2, g, wi, cw, cb, wo)


def _mem_kv_kernel(m_ref, g_ref, w_ref, kg_ref, k_ref, v_ref):
    hn = _rms_rows(m_ref[...], g_ref[0])
    y = _dot(hn.astype(BF16), w_ref[0])
    ks = []
    for h in range(X_HEADS):
        ks.append(_rms_rows(y[:, X_HEAD_DIM * h:X_HEAD_DIM * (h + 1)], kg_ref[0]))
    k_ref[0] = jnp.concatenate(ks, axis=1).astype(BF16)
    v_ref[0] = y[:, X_WIDTH:2 * X_WIDTH].astype(BF16)


def _mem_kv(mem2, g, w, kg):
    depth = w.shape[0]
    NM, D = mem2.shape
    return pl.pallas_call(
        _mem_kv_kernel,
        grid=(depth,),
        in_specs=[_resident((NM, D)), pl.BlockSpec((1, 1, D), lambda i: (i, 0, 0)),
                  pl.BlockSpec((1, D, 2 * X_WIDTH), lambda i: (i, 0, 0)),
                  pl.BlockSpec((1, 1, X_HEAD_DIM), lambda i: (i, 0, 0))],
        out_specs=[pl.BlockSpec((1, NM, X_WIDTH), lambda i: (i, 0, 0)),
                   pl.BlockSpec((1, NM, X_WIDTH), lambda i: (i, 0, 0))],
        out_shape=[jax.ShapeDtypeStruct((depth, NM, X_WIDTH), BF16)] * 2,
        compiler_params=_cparams(1),
        name="mem_kv",
    )(mem2, g, w, kg)


def _xattn_kernel(*refs, mixer_out):
    if mixer_out:
        x_ref, ya_ref, yb_ref, wm_ref, g_ref, wq_ref, qg_ref, k_ref, v_ref, wo_ref, o_ref = refs
        na = ya_ref.shape[1]
        x = x_ref[...] + _dot(ya_ref[...], wm_ref[0:na, :]) + _dot(yb_ref[...], wm_ref[na:, :])
    else:
        x_ref, g_ref, wq_ref, qg_ref, k_ref, v_ref, wo_ref, o_ref = refs
        x = x_ref[...]
    hn = _rms_rows(x, g_ref[...])
    q = _dot(hn.astype(BF16), wq_ref[...])
    k = k_ref[0]
    v = v_ref[0]
    heads = [slice(X_HEAD_DIM * h, X_HEAD_DIM * (h + 1)) for h in range(X_HEADS)]
    scores = [_dot_nt((_rms_rows(q[:, sl], qg_ref[...]) * (X_HEAD_DIM ** -0.5)).astype(BF16), k[:, sl])
              for sl in heads]
    outs = []
    for sl, s in zip(heads, scores):
        p = jnp.exp(s - jnp.max(s, axis=-1, keepdims=True))
        outs.append(_dot(p.astype(BF16), v[:, sl]) / jnp.sum(p, axis=-1, keepdims=True))
    o_ref[...] = x + _dot(jnp.concatenate(outs, axis=1).astype(BF16), wo_ref[...])


def _xattn(x2, mixer_out, g, wq, qg, k_all, v_all, wo, layer, B, T, tm, M):
    N, D = x2.shape
    nt = T // tm
    row = lambda b, t: (b * nt + t, 0)
    mrow = lambda b, t: (layer * B + b, 0, 0)
    mixer_args, mixer_specs = (), []
    if mixer_out is not None:
        ya, yb_t, w_mix = mixer_out
        mixer_args = (ya, yb_t, w_mix)
        mixer_specs = [pl.BlockSpec((tm, ya.shape[1]), row),
                       pl.BlockSpec((tm, w_mix.shape[0] - ya.shape[1]), lambda b, t: (t, b)), _resident(w_mix.shape)]
    return pl.pallas_call(
        functools.partial(_xattn_kernel, mixer_out=mixer_out is not None),
        grid=(B, nt),
        in_specs=[pl.BlockSpec((tm, D), row)] + mixer_specs + [
            _resident((1, D)), _resident(wq.shape), _resident((1, X_HEAD_DIM)),
            pl.BlockSpec((1, M, X_WIDTH), mrow), pl.BlockSpec((1, M, X_WIDTH), mrow), _resident(wo.shape)],
        out_specs=pl.BlockSpec((tm, D), row),
        out_shape=jax.ShapeDtypeStruct((N, D), F32),
        compiler_params=_cparams(2),
        name="mem_xattn",
    )(x2, *mixer_args, g, wq, qg, k_all, v_all, wo)


def _ffn_kernel(x_ref, g_ref, wu_ref, cw_ref, cb_ref, wd_ref, o_ref, halo_s):
    F = cw_ref.shape[1]
    x = x_ref[...]
    hn = _rms_rows(x, g_ref[...]).astype(BF16)
    _reset_halo(halo_s, pl.program_id(1) == 0)
    acc = x
    starts = list(range(0, F, MLP_CHUNK))

    def up_proj(c0):
        return _dot(hn, wu_ref[:, c0:c0 + MLP_CHUNK]), _dot(hn, wu_ref[:, F + c0:F + c0 + MLP_CHUNK])

    nxt = up_proj(0)
    for n, c0 in enumerate(starts):
        gate, up = nxt
        if n + 1 < len(starts):
            nxt = up_proj(starts[n + 1])
        cols = slice(c0, c0 + MLP_CHUNK)
        conv = _causal_conv3(gate, halo_s, cw_ref, cb_ref, cols)
        act = conv * (1.0 / (1.0 + jnp.exp(-conv))) * up
        acc = acc + _dot(act.astype(BF16), wd_ref[cols, :])
    o_ref[...] = acc


def _ffn(x2, g, wu, cw, cb, wd, B, T, tm):
    N, D = x2.shape
    F = cw.shape[1]
    nt = T // tm
    row = lambda b, t: (b * nt + t, 0)
    return pl.pallas_call(
        _ffn_kernel,
        grid=(B, nt),
        in_specs=[pl.BlockSpec((tm, D), row), _resident((1, D)),
                  pl.BlockSpec(wu.shape, lambda b, t: (0, 0), pipeline_mode=pl.Buffered(1)),
                  _resident(cw.shape), _resident(cb.shape),
                  pl.BlockSpec(wd.shape, lambda b, t: (0, 0), pipeline_mode=pl.Buffered(1))],
        out_specs=pl.BlockSpec((tm, D), row),
        out_shape=jax.ShapeDtypeStruct((N, D), F32),
        scratch_shapes=[pltpu.VMEM((tm + CONV_HALO, F), F32)],
        compiler_params=_cparams(2),
        name="conv_glu_ffn",
    )(x2, g, wu, cw, cb, wd)


def _s5_operators(lam_re, lam_im, log_step, b_re, b_im, c_re, c_im):
    G, P = lam_re.shape
    C = b_re.shape[-1]
    S = S5_GROUPS_PER_SLAB
    n_slab = G // S
    delta = jnp.exp(log_step)[:, None]
    mag = jnp.exp(lam_re * delta)
    bar_re = mag * jnp.cos(lam_im * delta)
    bar_im = mag * jnp.sin(lam_im * delta)
    den = lam_re * lam_re + lam_im * lam_im
    coef_re = ((bar_re - 1.0) * lam_re + bar_im * lam_im) / den
    coef_im = (bar_im * lam_re - (bar_re - 1.0) * lam_im) / den
    bb_re = coef_re[..., None] * b_re - coef_im[..., None] * b_im
    bb_im = coef_re[..., None] * b_im + coef_im[..., None] * b_re
    eye = jnp.eye(S, dtype=F32)

    def in_blocks(m):
        return jnp.einsum('kgpc,gh->kgchp', m.reshape(n_slab, S, P, C), eye).reshape(n_slab, S * C, S * P)

    def out_blocks(m):
        return jnp.einsum('kgcp,gh->kgphc', m.reshape(n_slab, S, C, P), eye).reshape(n_slab, S * P, S * C)

    bm = jnp.concatenate([in_blocks(bb_re), in_blocks(bb_im)], axis=2)
    cm = jnp.concatenate([out_blocks(c_re), -out_blocks(c_im)], axis=1)
    a_re = bar_re.reshape(1, G * P)
    a_im = bar_im.reshape(1, G * P)
    return bm.astype(BF16), cm.astype(BF16), a_re, a_im


def _even_w_in_layout(w):
    n_front = A_WIDTH + 2 * A_HEAD_DIM + IDX_HEADS * IDX_DIM + IDX_DIM + IDX_HEADS
    pad = jnp.zeros((w.shape[0], 128 - IDX_DIM - IDX_HEADS), w.dtype)
    return jnp.concatenate([w[:, :n_front], pad, w[:, n_front:]], axis=1).astype(BF16)


def kernel(x, mem, norm_mix, norm_x, norm_mem, norm_ffn, even_w_in, even_w_out, a_q_norm, a_k_norm,
           s5_lam_re, s5_lam_im, s5_log_step, s5_b_re, s5_b_im, s5_c_re, s5_c_im, s5_d, s5_w_glu, s5_b_glu,
           odd_w_in, odd_conv_w, odd_conv_b, odd_w_out, x_w_q, x_w_kv, x_w_o, x_q_norm, x_k_norm,
           f_w_up, f_conv_w, f_conv_b, f_w_down):
    B, T, D = x.shape
    M = mem.shape[1]
    depth = norm_mix.shape[0]
    assert D == 1024 and T % KEY_SUPER == 0 and even_w_in.shape[2] == 1476
    tm_proj, tm_odd, tm_ffn, s5_steps = min(1024, T), 512, 256, 64
    assert T % tm_proj == 0 and T % Q_BLOCK == 0

    x2 = x.reshape(B * T, D)
    mem2 = mem.reshape(B * M, D)

    k_all, v_all = _mem_kv(mem2, norm_mem.reshape(depth, 1, D), x_w_kv.astype(BF16),
                           x_k_norm.reshape(depth, 1, X_HEAD_DIM))
    k_all = k_all.reshape(depth * B, M, X_WIDTH)
    v_all = v_all.reshape(depth * B, M, X_WIDTH)

    head_mean = jnp.kron(jnp.eye(A_HEADS, dtype=F32),
                         jnp.full((A_HEAD_DIM, A_HEAD_DIM), 1.0 / A_HEAD_DIM, F32)).astype(BF16)
    upper_tri = jnp.triu(jnp.ones((KEY_TILE, KEY_TILE), F32)).astype(BF16)

    for i in range(depth):
        j = i // 2
        g_mix = norm_mix[i].reshape(1, D)
        if i % 2 == 0:
            q, kv, iq, ikw, u_t = _even_in(x2, g_mix, _even_w_in_layout(even_w_in[j]), B, T, tm_proj)
            qg = jnp.tile(a_q_norm[j], A_HEADS).reshape(1, A_WIDTH)
            kg = jnp.concatenate([a_k_norm[j], jnp.zeros((128 - A_HEAD_DIM,), F32)]).reshape(1, 128)
            ya = _dsa(q, iq, ikw, kv, qg, kg, head_mean, upper_tri, B, T, Q_BLOCK)
            bm, cm, a_re, a_im = _s5_operators(s5_lam_re[j], s5_lam_im[j], s5_log_step[j],
                                               s5_b_re[j], s5_b_im[j], s5_c_re[j], s5_c_im[j])
            yb_t = _s5(u_t, bm, cm, a_re, a_im, s5_d[j].reshape(1, 512),
                        s5_w_glu[j].astype(BF16), s5_b_glu[j].reshape(1, 512), B, T, s5_steps)
            mixer_out = (ya, yb_t, even_w_out[j].astype(BF16))
        else:
            x2 = _odd_mixer(x2, g_mix, odd_w_in[j].astype(BF16), odd_conv_w[j],
                            odd_conv_b[j].reshape(1, D), odd_w_out[j].astype(BF16), B, T, tm_odd)
            mixer_out = None
        x2 = _xattn(x2, mixer_out, norm_x[i].reshape(1, D), x_w_q[i].astype(BF16),
                    x_q_norm[i].reshape(1, X_HEAD_DIM), k_all, v_all, x_w_o[i].astype(BF16), i, B, T, tm_proj, M)
        x2 = _ffn(x2, norm_ffn[i].reshape(1, D), f_w_up[i].astype(BF16), f_conv_w[i],
                  f_conv_b[i].reshape(1, -1), f_w_down[i].astype(BF16), B, T, tm_ffn)
    return x2.reshape(B, T, D)
```

```python
import functools
import math

import jax
import jax.numpy as jnp
from jax import lax
from jax.experimental import pallas as pl
from jax.experimental.pallas import tpu as pltpu

F32 = jnp.float32
BF16 = jnp.bfloat16
I32 = jnp.int32

EPS = 1e-6
CHUNK = 64
Q_BLOCK = 512
KEY_TILE = 128
TILES_PER_SUPER = 4
KEY_SUPER = KEY_TILE * TILES_PER_SUPER
KEY_BITS = 16
KEY_MIN = -(1 << (KEY_BITS - 1))
V_ROWS = 80
LOG2_E = 1.4426950408889634
SHIFT_SLACK = 1.02
MAX_SAFE_SHIFT = 50.0
A_HEADS = 8
A_HEAD_DIM = 64
A_WIDTH = A_HEADS * A_HEAD_DIM
IDX_HEADS = 4
IDX_DIM = 64
TOPK_MAX = 256
S5_GROUPS_PER_SLAB = 8
X_HEADS = 4
X_HEAD_DIM = 128
X_WIDTH = X_HEADS * X_HEAD_DIM
CONV_HALO = 8
MLP_CHUNK = 256
ROW_SUB = 256
ODD_CHUNK = 256

V7X_VMEM_LIMIT_BYTES = 56 * 1024 * 1024
NEG_INF = float("-inf")


def _cparams(n_axes):
    return pltpu.CompilerParams(
        dimension_semantics=("arbitrary",) * n_axes,
        vmem_limit_bytes=V7X_VMEM_LIMIT_BYTES)


def _rms_rows(xf, g):
    ms = jnp.mean(xf * xf, axis=-1, keepdims=True)
    return xf * lax.rsqrt(ms + EPS) * g


def _dot(a, b):
    return jnp.dot(a, b, preferred_element_type=F32)


def _dot_nt(a, b):
    return lax.dot_general(a, b, (((1,), (1,)), ((), ())), preferred_element_type=F32)


def _resident(shape):
    nd = len(shape)
    return pl.BlockSpec(shape, lambda *_: (0,) * nd)


def _even_in_kernel(x_ref, g_ref, w_ref, q_ref, kv_ref, iq_ref, ikw_ref, u_ref):
    hn = _rms_rows(x_ref[...], g_ref[...])
    y = _dot(hn.astype(BF16), w_ref[...])
    q_ref[...] = y[:, 0:512].astype(BF16)
    kv_ref[...] = y[:, 512:640].astype(BF16)
    iq_ref[...] = y[:, 640:896].astype(BF16)
    ikw_ref[...] = y[:, 896:1024]
    u_ref[...] = y[:, 1024:1536].astype(BF16)


def _even_in(x2, g, w, B, T, tm):
    N, D = x2.shape
    nt = T // tm
    row = lambda b, t: (b * nt + t, 0)
    return pl.pallas_call(
        _even_in_kernel,
        grid=(B, nt),
        in_specs=[pl.BlockSpec((tm, D), row), _resident((1, D)), _resident(w.shape)],
        out_specs=[pl.BlockSpec((tm, 512), row), pl.BlockSpec((tm, 128), row),
                   pl.BlockSpec((tm, 256), row), pl.BlockSpec((tm, 128), row),
                   pl.BlockSpec((tm, 512), lambda b, t: (t, b))],
        out_shape=[jax.ShapeDtypeStruct((N, 512), BF16), jax.ShapeDtypeStruct((N, 128), BF16),
                   jax.ShapeDtypeStruct((N, 256), BF16), jax.ShapeDtypeStruct((N, 128), F32),
                   jax.ShapeDtypeStruct((T, B * 512), BF16)],
        compiler_params=_cparams(2),
        name="even_in",
    )(x2, g, w)


def _dsa_kernel(q_ref, iq_ref, iwq_ref, kv_ref, ikf_ref, qg_ref, kg_ref, bd_ref, ut_ref, o_ref,
                k1_s, ik2_s, vt_s, qs_s, iqs_s, wt_s, key_s, bias_s, m_s, acc_s, *, topk, QB):
    T = kv_ref.shape[0]
    qb = pl.program_id(1)
    nst = ((qb + 1) * (QB // KEY_TILE) + TILES_PER_SUPER - 1) >> (TILES_PER_SUPER.bit_length() - 1)

    qlane = lax.broadcasted_iota(I32, (KEY_TILE, QB), 1)
    row = lax.broadcasted_iota(I32, (KEY_TILE, QB), 0)
    lane = lax.broadcasted_iota(I32, (KEY_TILE, 128), 1)
    lo_half = lane < A_HEAD_DIM
    shift_lane = lane == A_HEAD_DIM

    q_scale = (A_HEAD_DIM ** -0.5) * LOG2_E
    shift = (A_HEAD_DIM * q_scale * SHIFT_SLACK) * jnp.max(jnp.abs(qg_ref[...])) * jnp.max(jnp.abs(kg_ref[...]))
    shift_is_safe = shift <= MAX_SAFE_SHIFT

    @pl.when(qb == 0)
    def _prepare_batch():
        ones_rows = jnp.where(lax.broadcasted_iota(I32, (V_ROWS - A_HEAD_DIM, KEY_TILE), 0) == 0, 1.0, 0.0)

        def key_tile(st, _):
            for t in range(TILES_PER_SUPER):
                off = pl.multiple_of(st * KEY_SUPER + t * KEY_TILE, KEY_TILE)
                kv = kv_ref[pl.ds(off, KEY_TILE), :].astype(F32)
                k = jnp.where(lo_half, kv, 0.0)
                ms = jnp.sum(k * k, axis=-1, keepdims=True) * (1.0 / A_HEAD_DIM)
                kn = k * lax.rsqrt(ms + EPS) * kg_ref[...]
                k1_s[pl.ds(off, KEY_TILE), :] = jnp.where(shift_lane, 1.0, kn).astype(BF16)
                ik = jnp.where(lo_half, ikf_ref[pl.ds(off, KEY_TILE), :], 0.0)
                ik2_s[pl.ds(off, KEY_TILE), :] = (ik + pltpu.roll(ik, 64, 1)).astype(BF16)
                v = pltpu.roll(jnp.where(lo_half, 0.0, kv), 64, 1)
                vt1 = jnp.concatenate([v.T[0:A_HEAD_DIM, :], ones_rows], axis=0)
                vt_s[st, :, t * KEY_TILE:(t + 1) * KEY_TILE] = vt1.astype(BF16)
            return 0

        lax.fori_loop(0, T // KEY_SUPER, key_tile, 0)

    q = q_ref[...].astype(F32)
    ms = _dot((q * q).astype(BF16), bd_ref[...])
    qn = q * lax.rsqrt(ms + EPS) * qg_ref[...] * q_scale
    half_q = lax.broadcasted_iota(I32, (QB, 128), 1) < A_HEAD_DIM
    neg_shift_lane = jnp.where(lax.broadcasted_iota(I32, (QB, 128), 1) == A_HEAD_DIM, -1.0, 0.0) * shift
    for h in range(A_HEADS):
        blk = qn[:, 128 * (h // 2):128 * (h // 2) + 128]
        if h % 2 == 1:
            blk = pltpu.roll(blk, 64, 1)
        qs_s[h * QB:(h + 1) * QB, :] = (jnp.where(half_q, blk, 0.0) + neg_shift_lane).astype(BF16)
    iq = iq_ref[...].astype(F32)
    for h in range(IDX_HEADS):
        blk = iq[:, 128 * (h // 2):128 * (h // 2) + 128]
        keep = half_q if h % 2 == 0 else jnp.logical_not(half_q)
        iqs_s[h * QB:(h + 1) * QB, :] = jnp.where(keep, blk, 0.0).astype(BF16)
    wt_s[...] = iwq_ref[...].T * ((IDX_DIM ** -0.5) * (IDX_HEADS ** -0.5))

    chunk_shift = CHUNK.bit_length() - 1
    limit = (((qb * QB + qlane) >> chunk_shift) + 1) << chunk_shift

    def score_tile(st, masked):
        off = pl.multiple_of(st * KEY_SUPER, KEY_SUPER)
        x = _dot_nt(ik2_s[pl.ds(off, KEY_SUPER), :], iqs_s[...])
        for t in range(TILES_PER_SUPER):
            rows = slice(t * KEY_TILE, (t + 1) * KEY_TILE)
            sc = wt_s[64:65, :] * jnp.maximum(x[rows, 0:QB], 0.0)
            for h in range(1, IDX_HEADS):
                sc = sc + wt_s[64 + h:65 + h, :] * jnp.maximum(x[rows, QB * h:QB * (h + 1)], 0.0)
            sc = jnp.where(sc == 0.0, 0.0, sc)
            if masked:
                sc = jnp.where(off + t * KEY_TILE + row < limit, sc, NEG_INF)
            bits = pltpu.bitcast(sc.astype(BF16).astype(F32), I32) >> 16
            key_s[pl.ds(off + t * KEY_TILE, KEY_TILE), :] = bits ^ ((bits >> 31) & 0x7FFF)

    def score_full_tile(st, carry):
        score_tile(st, False)
        return carry

    lax.fori_loop(0, nst - 1, score_full_tile, 0)
    score_tile(nst - 1, True)

    def count_ge(cand):
        def body(st, acc):
            off = pl.multiple_of(st * KEY_SUPER, KEY_SUPER)
            parts = []
            for t in range(TILES_PER_SUPER):
                ge = jnp.where(key_s[pl.ds(off + t * KEY_TILE, KEY_TILE), :] >= cand, 1, 0)
                parts.append(jnp.sum(ge.reshape(KEY_TILE // 8, 8, QB), axis=0))
            return acc + ((parts[0] + parts[1]) + (parts[2] + parts[3]))
        acc = lax.fori_loop(0, nst, body, jnp.zeros((8, QB), I32))
        return jnp.sum(acc, axis=0, keepdims=True)

    def bit_step(i, carry):
        prefix, cnt_ge = carry
        cand_u = prefix | lax.shift_left(jnp.int32(1), KEY_BITS - 1 - i)
        cnt = count_ge(cand_u + KEY_MIN)
        ok = cnt >= topk
        return jnp.where(ok, cand_u, prefix), jnp.where(ok, cnt, cnt_ge)

    prefix, cnt_ge = lax.fori_loop(
        0, KEY_BITS, bit_step,
        (jnp.zeros((1, QB), I32), jnp.zeros((1, QB), I32) + nst * KEY_SUPER))
    thresh = prefix + KEY_MIN
    excess = (cnt_ge - topk).astype(F32)

    def mask_tile(st, later_ties, masked):
        off = pl.multiple_of(st * KEY_SUPER, KEY_SUPER)
        for t in reversed(range(TILES_PER_SUPER)):
            o = off + t * KEY_TILE
            key = key_s[pl.ds(o, KEY_TILE), :]
            eq = key == thresh
            eq_f = jnp.where(eq, 1.0, 0.0)
            suffix = later_ties + _dot(ut_ref[...], eq_f.astype(BF16))
            tie_bias = jnp.where(suffix > excess, 0.0, NEG_INF)
            bias = jnp.where(key > thresh, 0.0, jnp.where(eq, tie_bias, NEG_INF))
            if masked:
                bias = jnp.where(o + row < limit, bias, NEG_INF)
            bias_s[pl.ds(o, KEY_TILE), :] = bias
            later_ties = later_ties + jnp.sum(eq_f, axis=0, keepdims=True)
        return later_ties

    ties_in_last = mask_tile(nst - 1, jnp.zeros((1, QB), F32), True)
    lax.fori_loop(0, nst - 1, lambda i, ties: mask_tile(nst - 2 - i, ties, False), ties_in_last)

    acc_s[...] = jnp.zeros(acc_s.shape, F32)

    @pl.when(shift_is_safe)
    def _attend_shifted():
        def scores(st):
            off = pl.multiple_of(st * KEY_SUPER, KEY_SUPER)
            return _dot_nt(k1_s[pl.ds(off, KEY_SUPER), :], qs_s[...])

        def attend(st, sp):
            off = pl.multiple_of(st * KEY_SUPER, KEY_SUPER)
            bias = bias_s[pl.ds(off, KEY_SUPER), :]
            p = jnp.concatenate(
                [jnp.exp2(sp[:, h * QB:(h + 1) * QB] + bias).astype(BF16) for h in range(A_HEADS)],
                axis=1)
            acc_s[...] += _dot(vt_s[st], p)

        def attend_pair(i, _):
            sp0 = scores(2 * i)
            sp1 = scores(2 * i + 1)
            attend(2 * i, sp0)
            attend(2 * i + 1, sp1)
            return 0

        lax.fori_loop(0, nst >> 1, attend_pair, 0)

        @pl.when((nst & 1) == 1)
        def _odd_tail():
            attend(nst - 1, scores(nst - 1))

    @pl.when(jnp.logical_not(shift_is_safe))
    def _attend_online():
        m_s[...] = jnp.full(m_s.shape, NEG_INF, F32)

        def attend_tile(st, _):
            off = pl.multiple_of(st * KEY_SUPER, KEY_SUPER)
            k_tile = k1_s[pl.ds(off, KEY_SUPER), :]
            bias = bias_s[pl.ds(off, KEY_SUPER), :]
            for h in range(A_HEADS):
                cols = slice(h * QB, (h + 1) * QB)
                s = _dot_nt(k_tile, qs_s[cols, :]) + bias
                m_old = m_s[h:h + 1, :]
                m_new = jnp.maximum(m_old, jnp.max(s, axis=0, keepdims=True))
                m_safe = jnp.where(m_new == NEG_INF, 0.0, m_new)
                p = jnp.exp2(s - m_safe).astype(BF16)
                acc_s[:, cols] = jnp.exp2(m_old - m_safe) * acc_s[:, cols] + _dot(vt_s[st], p)
                m_s[h:h + 1, :] = m_new
            return 0

        lax.fori_loop(0, nst, attend_tile, 0)

    out_t = jnp.concatenate(
        [acc_s[0:A_HEAD_DIM, h * QB:(h + 1) * QB] / acc_s[A_HEAD_DIM:A_HEAD_DIM + 1, h * QB:(h + 1) * QB]
         for h in range(A_HEADS)], axis=0)
    o_ref[...] = out_t.T.astype(BF16)


def _dsa(q, iq, ikw, kv, qg, kg, bd, ut, B, T, QB):
    N = q.shape[0]
    nqb = T // QB
    topk = min(TOPK_MAX, T // 4)
    qrow = lambda b, j: (b * nqb + j, 0)
    brow = lambda b, j: (b, 0)
    return pl.pallas_call(
        functools.partial(_dsa_kernel, topk=topk, QB=QB),
        grid=(B, nqb),
        in_specs=[pl.BlockSpec((QB, 512), qrow), pl.BlockSpec((QB, 256), qrow),
                  pl.BlockSpec((QB, 128), qrow), pl.BlockSpec((T, 128), brow),
                  pl.BlockSpec((T, 128), brow), _resident((1, 512)), _resident((1, 128)),
                  _resident((512, 512)), _resident((KEY_TILE, KEY_TILE))],
        out_specs=pl.BlockSpec((QB, 512), qrow),
        out_shape=jax.ShapeDtypeStruct((N, 512), BF16),
        scratch_shapes=[
            pltpu.VMEM((T, 128), BF16),
            pltpu.VMEM((T, 128), BF16),
            pltpu.VMEM((T // KEY_SUPER, V_ROWS, KEY_SUPER), BF16),
            pltpu.VMEM((A_HEADS * QB, 128), BF16),
            pltpu.VMEM((IDX_HEADS * QB, 128), BF16),
            pltpu.VMEM((128, QB), F32),
            pltpu.VMEM((T, QB), I32),
            pltpu.VMEM((T, QB), F32),
            pltpu.VMEM((A_HEADS, QB), F32),
            pltpu.VMEM((V_ROWS, A_HEADS * QB), F32),
        ],
        compiler_params=_cparams(2),
        name="dsa_attention",
    )(q, iq, ikw, kv, ikw, qg, kg, bd, ut)


def _gelu_tanh(x):
    return 0.5 * x * (1.0 + jnp.tanh(math.sqrt(2.0 / math.pi) * (x + 0.044715 * (x * x * x))))


def _s5_kernel(u_ref, bm_ref, cm_ref, are_ref, aim_ref, d_ref, wg_ref, bg_ref, o_ref, buf_s, h_s, tm_s,
               *, B, steps):
    n_slab = bm_ref.shape[0]
    half = bm_ref.shape[2] // 2
    width = 2 * half
    n_lane_slabs = tm_s.shape[0]

    @pl.when(pl.program_id(0) == 0)
    def _init():
        h_s[...] = jnp.zeros(h_s.shape, F32)

    for b in range(B):
        for j in range(n_lane_slabs):
            c = b * 128 * n_lane_slabs + 128 * j
            tm_s[j, pl.ds(b, steps, stride=B), :] = u_ref[:, c:c + 128].astype(F32)
    u32 = jnp.concatenate([tm_s[j] for j in range(n_lane_slabs)], axis=1)
    u = u32.astype(BF16)

    def expand(k):
        buf_s[:, width * k:width * (k + 1)] = _dot(u[:, 128 * k:128 * (k + 1)], bm_ref[k])

    ys = []
    expand(0)
    for k in range(n_slab):
        if k + 1 < n_slab:
            expand(k + 1)
        c0 = width * k
        ar = jnp.broadcast_to(are_ref[:, half * k:half * (k + 1)], (B, half))
        ai = jnp.broadcast_to(aim_ref[:, half * k:half * (k + 1)], (B, half))
        hr, hi = h_s[:, c0:c0 + half], h_s[:, c0 + half:c0 + width]
        for t in range(steps):
            rows = slice(t * B, (t + 1) * B)
            hr, hi = (ar * hr - ai * hi + buf_s[rows, c0:c0 + half],
                      ar * hi + ai * hr + buf_s[rows, c0 + half:c0 + width])
            buf_s[rows, c0:c0 + half] = hr
            buf_s[rows, c0 + half:c0 + width] = hi
        h_s[:, c0:c0 + half] = hr
        h_s[:, c0 + half:c0 + width] = hi
        ys.append(_dot(buf_s[:, c0:c0 + width].astype(BF16), cm_ref[k]))
    y = jnp.concatenate(ys, axis=1)
    y = _gelu_tanh(y + d_ref[...] * u32)
    z = _dot(y.astype(BF16), wg_ref[...]) + bg_ref[...]
    out = y * (1.0 / (1.0 + jnp.exp(-z)))
    for j in range(n_lane_slabs):
        tm_s[j] = out[:, 128 * j:128 * (j + 1)]
    for b in range(B):
        for j in range(n_lane_slabs):
            c = b * 128 * n_lane_slabs + 128 * j
            o_ref[:, c:c + 128] = tm_s[j, pl.ds(b, steps, stride=B), :].astype(BF16)


def _s5(u_bt, bm, cm, a_re, a_im, d, wg, bg, B, T, steps):
    rows = steps * B
    width = u_bt.shape[1] // B
    n_state = bm.shape[0] * bm.shape[2]
    return pl.pallas_call(
        functools.partial(_s5_kernel, B=B, steps=steps),
        grid=(T // steps,),
        in_specs=[pl.BlockSpec((steps, B * width), lambda i: (i, 0)), _resident(bm.shape), _resident(cm.shape),
                  _resident(a_re.shape), _resident(a_im.shape), _resident(d.shape),
                  _resident(wg.shape), _resident(bg.shape)],
        out_specs=pl.BlockSpec((steps, B * width), lambda i: (i, 0)),
        out_shape=jax.ShapeDtypeStruct((T, B * width), BF16),
        scratch_shapes=[pltpu.VMEM((rows, n_state), F32),
                        pltpu.VMEM((B, n_state), F32),
                        pltpu.VMEM((width // 128, rows, 128), F32)],
        compiler_params=_cparams(1),
        name="s5_scan",
    )(u_bt, bm, cm, a_re, a_im, d, wg, bg)


def _reset_halo(halo_s, first_tile):
    @pl.when(first_tile)
    def _zero_halo():
        halo_s[0:CONV_HALO, :] = jnp.zeros((CONV_HALO, halo_s.shape[1]), F32)


def _causal_conv3(c, halo_s, w_ref, b_ref, cols):
    tm = c.shape[0]
    halo_s[CONV_HALO:CONV_HALO + tm, cols] = c
    c1 = halo_s[CONV_HALO - 1:CONV_HALO - 1 + tm, cols]
    c2 = halo_s[CONV_HALO - 2:CONV_HALO - 2 + tm, cols]
    y = w_ref[0:1, cols] * c2 + w_ref[1:2, cols] * c1 + w_ref[2:3, cols] * c + b_ref[:, cols]
    halo_s[0:CONV_HALO, cols] = halo_s[tm:tm + CONV_HALO, cols]
    return y


def _odd_kernel(x_ref, g_ref, wi_ref, cw_ref, cb_ref, wo_ref, o_ref, halo_s):
    D = x_ref.shape[1]
    tm = x_ref.shape[0]
    _reset_halo(halo_s, pl.program_id(1) == 0)
    work = [(r0, c0) for r0 in range(0, tm, ROW_SUB) for c0 in range(0, D, ODD_CHUNK)]
    normed = {}

    def in_proj(r0, c0):
        if r0 not in normed:
            normed[r0] = _rms_rows(x_ref[r0:r0 + ROW_SUB, :], g_ref[...]).astype(BF16)
        hn = normed[r0]
        return tuple(_dot(hn, wi_ref[:, part * D + c0:part * D + c0 + ODD_CHUNK]) for part in range(3))

    nxt = in_proj(*work[0])
    acc = None
    for n, (r0, c0) in enumerate(work):
        gb, gc, z = nxt
        if n + 1 < len(work):
            nxt = in_proj(*work[n + 1])
        if c0 == 0:
            acc = x_ref[r0:r0 + ROW_SUB, :]
        cols = slice(c0, c0 + ODD_CHUNK)
        conv = _causal_conv3(gc * z, halo_s, cw_ref, cb_ref, cols)
        acc = acc + _dot((gb * conv).astype(BF16), wo_ref[cols, :])
        if c0 + ODD_CHUNK >= D:
            o_ref[r0:r0 + ROW_SUB, :] = acc


def _odd_mixer(x2, g, wi, cw, cb, wo, B, T, tm):
    N, D = x2.shape
    nt = T // tm
    row = lambda b, t: (b * nt + t, 0)
    return pl.pallas_call(
        _odd_kernel,
        grid=(B, nt),
        in_specs=[pl.BlockSpec((tm, D), row), _resident((1, D)), _resident(wi.shape),
                  _resident(cw.shape), _resident(cb.shape), _resident(wo.shape)],
        out_specs=pl.BlockSpec((tm, D), row),
        out_shape=jax.ShapeDtypeStruct((N, D), F32),
        scratch_shapes=[pltpu.VMEM((ROW_SUB + CONV_HALO, D), F32)],
        compiler_params=_cparams(2),
        name="odd_mixer",
    )(x2, g, wi, cw, cb, wo)


def _mem_kv_kernel(m_ref, g_ref, w_ref, kg_ref, k_ref, v_ref):
    hn = _rms_rows(m_ref[...], g_ref[0])
    y = _dot(hn.astype(BF16), w_ref[0])
    ks = []
    for h in range(X_HEADS):
        ks.append(_rms_rows(y[:, X_HEAD_DIM * h:X_HEAD_DIM * (h + 1)], kg_ref[0]))
    k_ref[0] = jnp.concatenate(ks, axis=1).astype(BF16)
    v_ref[0] = y[:, X_WIDTH:2 * X_WIDTH].astype(BF16)


def _mem_kv(mem2, g, w, kg):
    depth = w.shape[0]
    NM, D = mem2.shape
    return pl.pallas_call(
        _mem_kv_kernel,
        grid=(depth,),
        in_specs=[_resident((NM, D)), pl.BlockSpec((1, 1, D), lambda i: (i, 0, 0)),
                  pl.BlockSpec((1, D, 2 * X_WIDTH), lambda i: (i, 0, 0)),
                  pl.BlockSpec((1, 1, X_HEAD_DIM), lambda i: (i, 0, 0))],
        out_specs=[pl.BlockSpec((1, NM, X_WIDTH), lambda i: (i, 0, 0)),
                   pl.BlockSpec((1, NM, X_WIDTH), lambda i: (i, 0, 0))],
        out_shape=[jax.ShapeDtypeStruct((depth, NM, X_WIDTH), BF16)] * 2,
        compiler_params=_cparams(1),
        name="mem_kv",
    )(mem2, g, w, kg)


def _xattn_kernel(*refs, mixer_out):
    if mixer_out:
        x_ref, ya_ref, yb_ref, wm_ref, g_ref, wq_ref, qg_ref, k_ref, v_ref, wo_ref, o_ref = refs
        na = ya_ref.shape[1]
        x = x_ref[...] + _dot(ya_ref[...], wm_ref[0:na, :]) + _dot(yb_ref[...], wm_ref[na:, :])
    else:
        x_ref, g_ref, wq_ref, qg_ref, k_ref, v_ref, wo_ref, o_ref = refs
        x = x_ref[...]
    hn = _rms_rows(x, g_ref[...])
    q = _dot(hn.astype(BF16), wq_ref[...])
    k = k_ref[0]
    v = v_ref[0]
    heads = [slice(X_HEAD_DIM * h, X_HEAD_DIM * (h + 1)) for h in range(X_HEADS)]
    scores = [_dot_nt((_rms_rows(q[:, sl], qg_ref[...]) * (X_HEAD_DIM ** -0.5)).astype(BF16), k[:, sl])
              for sl in heads]
    outs = []
    for sl, s in zip(heads, scores):
        p = jnp.exp(s - jnp.max(s, axis=-1, keepdims=True))
        outs.append(_dot(p.astype(BF16), v[:, sl]) / jnp.sum(p, axis=-1, keepdims=True))
    o_ref[...] = x + _dot(jnp.concatenate(outs, axis=1).astype(BF16), wo_ref[...])


def _xattn(x2, mixer_out, g, wq, qg, k_all, v_all, wo, layer, B, T, tm, M):
    N, D = x2.shape
    nt = T // tm
    row = lambda b, t: (b * nt + t, 0)
    mrow = lambda b, t: (layer * B + b, 0, 0)
    mixer_args, mixer_specs = (), []
    if mixer_out is not None:
        ya, yb_t, w_mix = mixer_out
        mixer_args = (ya, yb_t, w_mix)
        mixer_specs = [pl.BlockSpec((tm, ya.shape[1]), row),
                       pl.BlockSpec((tm, w_mix.shape[0] - ya.shape[1]), lambda b, t: (t, b)), _resident(w_mix.shape)]
    return pl.pallas_call(
        functools.partial(_xattn_kernel, mixer_out=mixer_out is not None),
        grid=(B, nt),
        in_specs=[pl.BlockSpec((tm, D), row)] + mixer_specs + [
            _resident((1, D)), _resident(wq.shape), _resident((1, X_HEAD_DIM)),
            pl.BlockSpec((1, M, X_WIDTH), mrow), pl.BlockSpec((1, M, X_WIDTH), mrow), _resident(wo.shape)],
        out_specs=pl.BlockSpec((tm, D), row),
        out_shape=jax.ShapeDtypeStruct((N, D), F32),
        compiler_params=_cparams(2),
        name="mem_xattn",
    )(x2, *mixer_args, g, wq, qg, k_all, v_all, wo)


def _ffn_kernel(x_ref, g_ref, wu_ref, cw_ref, cb_ref, wd_ref, o_ref, halo_s):
    F = cw_ref.shape[1]
    tm = x_ref.shape[0]
    _reset_halo(halo_s, pl.program_id(1) == 0)
    work = [(r0, c0) for r0 in range(0, tm, ROW_SUB) for c0 in range(0, F, MLP_CHUNK)]
    normed = {}

    def up_proj(r0, c0):
        if r0 not in normed:
            normed[r0] = _rms_rows(x_ref[r0:r0 + ROW_SUB, :], g_ref[...]).astype(BF16)
        hn = normed[r0]
        return _dot(hn, wu_ref[:, c0:c0 + MLP_CHUNK]), _dot(hn, wu_ref[:, F + c0:F + c0 + MLP_CHUNK])

    nxt = up_proj(*work[0])
    acc = None
    for n, (r0, c0) in enumerate(work):
        gate, up = nxt
        if n + 1 < len(work):
            nxt = up_proj(*work[n + 1])
        if c0 == 0:
            acc = x_ref[r0:r0 + ROW_SUB, :]
        cols = slice(c0, c0 + MLP_CHUNK)
        conv = _causal_conv3(gate, halo_s, cw_ref, cb_ref, cols)
        act = conv * (1.0 / (1.0 + jnp.exp(-conv))) * up
        acc = acc + _dot(act.astype(BF16), wd_ref[cols, :])
        if c0 + MLP_CHUNK >= F:
            o_ref[r0:r0 + ROW_SUB, :] = acc


def _ffn(x2, g, wu, cw, cb, wd, B, T, tm):
    N, D = x2.shape
    F = cw.shape[1]
    nt = T // tm
    row = lambda b, t: (b * nt + t, 0)
    return pl.pallas_call(
        _ffn_kernel,
        grid=(B, nt),
        in_specs=[pl.BlockSpec((tm, D), row), _resident((1, D)),
                  pl.BlockSpec(wu.shape, lambda b, t: (0, 0), pipeline_mode=pl.Buffered(1)),
                  _resident(cw.shape), _resident(cb.shape),
                  pl.BlockSpec(wd.shape, lambda b, t: (0, 0), pipeline_mode=pl.Buffered(1))],
        out_specs=pl.BlockSpec((tm, D), row),
        out_shape=jax.ShapeDtypeStruct((N, D), F32),
        scratch_shapes=[pltpu.VMEM((ROW_SUB + CONV_HALO, F), F32)],
        compiler_params=_cparams(2),
        name="conv_glu_ffn",
    )(x2, g, wu, cw, cb, wd)


def _s5_operators(lam_re, lam_im, log_step, b_re, b_im, c_re, c_im):
    G, P = lam_re.shape
    C = b_re.shape[-1]
    S = S5_GROUPS_PER_SLAB
    n_slab = G // S
    delta = jnp.exp(log_step)[:, None]
    mag = jnp.exp(lam_re * delta)
    bar_re = mag * jnp.cos(lam_im * delta)
    bar_im = mag * jnp.sin(lam_im * delta)
    den = lam_re * lam_re + lam_im * lam_im
    coef_re = ((bar_re - 1.0) * lam_re + bar_im * lam_im) / den
    coef_im = (bar_im * lam_re - (bar_re - 1.0) * lam_im) / den
    bb_re = coef_re[..., None] * b_re - coef_im[..., None] * b_im
    bb_im = coef_re[..., None] * b_im + coef_im[..., None] * b_re
    eye = jnp.eye(S, dtype=F32)

    def in_blocks(m):
        return jnp.einsum('kgpc,gh->kgchp', m.reshape(n_slab, S, P, C), eye).reshape(n_slab, S * C, S * P)

    def out_blocks(m):
        return jnp.einsum('kgcp,gh->kgphc', m.reshape(n_slab, S, C, P), eye).reshape(n_slab, S * P, S * C)

    bm = jnp.concatenate([in_blocks(bb_re), in_blocks(bb_im)], axis=2)
    cm = jnp.concatenate([out_blocks(c_re), -out_blocks(c_im)], axis=1)
    a_re = bar_re.reshape(1, G * P)
    a_im = bar_im.reshape(1, G * P)
    return bm.astype(BF16), cm.astype(BF16), a_re, a_im


def _even_w_in_layout(w):
    n_front = A_WIDTH + 2 * A_HEAD_DIM + IDX_HEADS * IDX_DIM + IDX_DIM + IDX_HEADS
    pad = jnp.zeros((w.shape[0], 128 - IDX_DIM - IDX_HEADS), w.dtype)
    return jnp.concatenate([w[:, :n_front], pad, w[:, n_front:]], axis=1).astype(BF16)


def kernel(x, mem, norm_mix, norm_x, norm_mem, norm_ffn, even_w_in, even_w_out, a_q_norm, a_k_norm,
           s5_lam_re, s5_lam_im, s5_log_step, s5_b_re, s5_b_im, s5_c_re, s5_c_im, s5_d, s5_w_glu, s5_b_glu,
           odd_w_in, odd_conv_w, odd_conv_b, odd_w_out, x_w_q, x_w_kv, x_w_o, x_q_norm, x_k_norm,
           f_w_up, f_conv_w, f_conv_b, f_w_down):
    B, T, D = x.shape
    M = mem.shape[1]
    depth = norm_mix.shape[0]
    assert D == 1024 and T % KEY_SUPER == 0 and even_w_in.shape[2] == 1476
    tm_proj, s5_steps = min(1024, T), 64
    tm_odd = tm_ffn = tm_proj
    assert T % tm_proj == 0 and T % Q_BLOCK == 0

    x2 = x.reshape(B * T, D)
    mem2 = mem.reshape(B * M, D)

    k_all, v_all = _mem_kv(mem2, norm_mem.reshape(depth, 1, D), x_w_kv.astype(BF16),
                           x_k_norm.reshape(depth, 1, X_HEAD_DIM))
    k_all = k_all.reshape(depth * B, M, X_WIDTH)
    v_all = v_all.reshape(depth * B, M, X_WIDTH)

    head_mean = jnp.kron(jnp.eye(A_HEADS, dtype=F32),
                         jnp.full((A_HEAD_DIM, A_HEAD_DIM), 1.0 / A_HEAD_DIM, F32)).astype(BF16)
    upper_tri = jnp.triu(jnp.ones((KEY_TILE, KEY_TILE), F32)).astype(BF16)

    for i in range(depth):
        j = i // 2
        g_mix = norm_mix[i].reshape(1, D)
        if i % 2 == 0:
            q, kv, iq, ikw, u_t = _even_in(x2, g_mix, _even_w_in_layout(even_w_in[j]), B, T, tm_proj)
            qg = jnp.tile(a_q_norm[j], A_HEADS).reshape(1, A_WIDTH)
            kg = jnp.concatenate([a_k_norm[j], jnp.zeros((128 - A_HEAD_DIM,), F32)]).reshape(1, 128)
            ya = _dsa(q, iq, ikw, kv, qg, kg, head_mean, upper_tri, B, T, Q_BLOCK)
            bm, cm, a_re, a_im = _s5_operators(s5_lam_re[j], s5_lam_im[j], s5_log_step[j],
                                               s5_b_re[j], s5_b_im[j], s5_c_re[j], s5_c_im[j])
            yb_t = _s5(u_t, bm, cm, a_re, a_im, s5_d[j].reshape(1, 512),
                        s5_w_glu[j].astype(BF16), s5_b_glu[j].reshape(1, 512), B, T, s5_steps)
            mixer_out = (ya, yb_t, even_w_out[j].astype(BF16))
        else:
            x2 = _odd_mixer(x2, g_mix, odd_w_in[j].astype(BF16), odd_conv_w[j],
                            odd_conv_b[j].reshape(1, D), odd_w_out[j].astype(BF16), B, T, tm_odd)
            mixer_out = None
        x2 = _xattn(x2, mixer_out, norm_x[i].reshape(1, D), x_w_q[i].astype(BF16),
                    x_q_norm[i].reshape(1, X_HEAD_DIM), k_all, v_all, x_w_o[i].astype(BF16), i, B, T, tm_proj, M)
        x2 = _ffn(x2, norm_ffn[i].reshape(1, D), f_w_up[i].astype(BF16), f_conv_w[i],
                  f_conv_b[i].reshape(1, -1), f_w_down[i].astype(BF16), B, T, tm_ffn)
    return x2.reshape(B, T, D)
```

```python
import functools
import math

import jax
import jax.numpy as jnp
from jax import lax
from jax.experimental import pallas as pl
from jax.experimental.pallas import tpu as pltpu

F32 = jnp.float32
BF16 = jnp.bfloat16
I32 = jnp.int32

EPS = 1e-6
CHUNK = 64
Q_BLOCK = 512
KEY_TILE = 128
TILES_PER_SUPER = 4
KEY_SUPER = KEY_TILE * TILES_PER_SUPER
KEY_BITS = 16
KEY_MIN = -(1 << (KEY_BITS - 1))
PAIR_GUARD = -2147450880
PAIR_ONES = 0x00010001
V_ROWS = 80
LOG2_E = 1.4426950408889634
SHIFT_SLACK = 1.02
MAX_SAFE_SHIFT = 50.0
A_HEADS = 8
A_HEAD_DIM = 64
A_WIDTH = A_HEADS * A_HEAD_DIM
IDX_HEADS = 4
IDX_DIM = 64
TOPK_MAX = 256
S5_GROUPS_PER_SLAB = 8
X_HEADS = 4
X_HEAD_DIM = 128
X_WIDTH = X_HEADS * X_HEAD_DIM
CONV_HALO = 8
MLP_CHUNK = 256
ROW_SUB = 256
ODD_CHUNK = 256

V7X_VMEM_LIMIT_BYTES = 56 * 1024 * 1024
NEG_INF = float("-inf")


def _cparams(n_axes):
    return pltpu.CompilerParams(
        dimension_semantics=("arbitrary",) * n_axes,
        vmem_limit_bytes=V7X_VMEM_LIMIT_BYTES)


def _rms_rows(xf, g):
    ms = jnp.mean(xf * xf, axis=-1, keepdims=True)
    return xf * lax.rsqrt(ms + EPS) * g


def _dot(a, b):
    return jnp.dot(a, b, preferred_element_type=F32)


def _dot_nt(a, b):
    return lax.dot_general(a, b, (((1,), (1,)), ((), ())), preferred_element_type=F32)


def _resident(shape):
    nd = len(shape)
    return pl.BlockSpec(shape, lambda *_: (0,) * nd)


def _even_in_kernel(x_ref, g_ref, w_ref, q_ref, kv_ref, iq_ref, ikw_ref, u_ref):
    hn = _rms_rows(x_ref[...], g_ref[...])
    y = _dot(hn.astype(BF16), w_ref[...])
    q_ref[...] = y[:, 0:512].astype(BF16)
    kv_ref[...] = y[:, 512:640].astype(BF16)
    iq_ref[...] = y[:, 640:896].astype(BF16)
    ikw_ref[...] = y[:, 896:1024]
    u_ref[...] = y[:, 1024:1536].astype(BF16)


def _even_in(x2, g, w, B, T, tm):
    N, D = x2.shape
    nt = T // tm
    row = lambda b, t: (b * nt + t, 0)
    return pl.pallas_call(
        _even_in_kernel,
        grid=(B, nt),
        in_specs=[pl.BlockSpec((tm, D), row), _resident((1, D)), _resident(w.shape)],
        out_specs=[pl.BlockSpec((tm, 512), row), pl.BlockSpec((tm, 128), row),
                   pl.BlockSpec((tm, 256), row), pl.BlockSpec((tm, 128), row),
                   pl.BlockSpec((tm, 512), lambda b, t: (t, b))],
        out_shape=[jax.ShapeDtypeStruct((N, 512), BF16), jax.ShapeDtypeStruct((N, 128), BF16),
                   jax.ShapeDtypeStruct((N, 256), BF16), jax.ShapeDtypeStruct((N, 128), F32),
                   jax.ShapeDtypeStruct((T, B * 512), BF16)],
        compiler_params=_cparams(2),
        name="even_in",
    )(x2, g, w)


def _dsa_kernel(q_ref, iq_ref, iwq_ref, kv_ref, ikf_ref, qg_ref, kg_ref, bd_ref, ut_ref, o_ref,
                k1_s, ik2_s, vt_s, qs_s, iqs_s, wt_s, key_s, pk_s, bias_s, m_s, acc_s, *, topk, QB):
    T = kv_ref.shape[0]
    qb = pl.program_id(1)
    nst = ((qb + 1) * (QB // KEY_TILE) + TILES_PER_SUPER - 1) >> (TILES_PER_SUPER.bit_length() - 1)

    qlane = lax.broadcasted_iota(I32, (KEY_TILE, QB), 1)
    row = lax.broadcasted_iota(I32, (KEY_TILE, QB), 0)
    lane = lax.broadcasted_iota(I32, (KEY_TILE, 128), 1)
    lo_half = lane < A_HEAD_DIM
    shift_lane = lane == A_HEAD_DIM

    q_scale = (A_HEAD_DIM ** -0.5) * LOG2_E
    shift = (A_HEAD_DIM * q_scale * SHIFT_SLACK) * jnp.max(jnp.abs(qg_ref[...])) * jnp.max(jnp.abs(kg_ref[...]))
    shift_is_safe = shift <= MAX_SAFE_SHIFT

    @pl.when(qb == 0)
    def _prepare_batch():
        ones_rows = jnp.where(lax.broadcasted_iota(I32, (V_ROWS - A_HEAD_DIM, KEY_TILE), 0) == 0, 1.0, 0.0)

        def key_tile(st, _):
            for t in range(TILES_PER_SUPER):
                off = pl.multiple_of(st * KEY_SUPER + t * KEY_TILE, KEY_TILE)
                kv = kv_ref[pl.ds(off, KEY_TILE), :].astype(F32)
                k = jnp.where(lo_half, kv, 0.0)
                ms = jnp.sum(k * k, axis=-1, keepdims=True) * (1.0 / A_HEAD_DIM)
                kn = k * lax.rsqrt(ms + EPS) * kg_ref[...]
                k1_s[pl.ds(off, KEY_TILE), :] = jnp.where(shift_lane, 1.0, kn).astype(BF16)
                ik = jnp.where(lo_half, ikf_ref[pl.ds(off, KEY_TILE), :], 0.0)
                ik2_s[pl.ds(off, KEY_TILE), :] = (ik + pltpu.roll(ik, 64, 1)).astype(BF16)
                v = pltpu.roll(jnp.where(lo_half, 0.0, kv), 64, 1)
                vt1 = jnp.concatenate([v.T[0:A_HEAD_DIM, :], ones_rows], axis=0)
                vt_s[st, :, t * KEY_TILE:(t + 1) * KEY_TILE] = vt1.astype(BF16)
            return 0

        lax.fori_loop(0, T // KEY_SUPER, key_tile, 0)

    q = q_ref[...].astype(F32)
    ms = _dot((q * q).astype(BF16), bd_ref[...])
    qn = q * lax.rsqrt(ms + EPS) * qg_ref[...] * q_scale
    half_q = lax.broadcasted_iota(I32, (QB, 128), 1) < A_HEAD_DIM
    neg_shift_lane = jnp.where(lax.broadcasted_iota(I32, (QB, 128), 1) == A_HEAD_DIM, -1.0, 0.0) * shift
    for h in range(A_HEADS):
        blk = qn[:, 128 * (h // 2):128 * (h // 2) + 128]
        if h % 2 == 1:
            blk = pltpu.roll(blk, 64, 1)
        qs_s[h * QB:(h + 1) * QB, :] = (jnp.where(half_q, blk, 0.0) + neg_shift_lane).astype(BF16)
    iq = iq_ref[...].astype(F32)
    for h in range(IDX_HEADS):
        blk = iq[:, 128 * (h // 2):128 * (h // 2) + 128]
        keep = half_q if h % 2 == 0 else jnp.logical_not(half_q)
        iqs_s[h * QB:(h + 1) * QB, :] = jnp.where(keep, blk, 0.0).astype(BF16)
    wt_s[...] = iwq_ref[...].T * ((IDX_DIM ** -0.5) * (IDX_HEADS ** -0.5))

    chunk_shift = CHUNK.bit_length() - 1
    limit = (((qb * QB + qlane) >> chunk_shift) + 1) << chunk_shift

    def score_tile(st, masked):
        off = pl.multiple_of(st * KEY_SUPER, KEY_SUPER)
        x = _dot_nt(ik2_s[pl.ds(off, KEY_SUPER), :], iqs_s[...])
        for t in range(TILES_PER_SUPER):
            rows = slice(t * KEY_TILE, (t + 1) * KEY_TILE)
            sc = wt_s[64:65, :] * jnp.maximum(x[rows, 0:QB], 0.0)
            for h in range(1, IDX_HEADS):
                sc = sc + wt_s[64 + h:65 + h, :] * jnp.maximum(x[rows, QB * h:QB * (h + 1)], 0.0)
            sc = jnp.where(sc == 0.0, 0.0, sc)
            if masked:
                sc = jnp.where(off + t * KEY_TILE + row < limit, sc, NEG_INF)
            bits = pltpu.bitcast(sc.astype(BF16).astype(F32), I32) >> 16
            key = bits ^ ((bits >> 31) & 0x7FFF)
            key_s[pl.ds(off + t * KEY_TILE, KEY_TILE), :] = key
            top15 = lax.shift_right_logical(key - KEY_MIN, 1)
            if t % 2 == 0:
                pair_hi = top15 << 16
            else:
                half_off = pl.multiple_of(st * (KEY_SUPER // 2), KEY_SUPER // 2)
                pk_s[pl.ds(half_off + (t // 2) * KEY_TILE, KEY_TILE), :] = pair_hi | top15 | PAIR_GUARD

    def score_full_tile(st, carry):
        score_tile(st, False)
        return carry

    lax.fori_loop(0, nst - 1, score_full_tile, 0)
    score_tile(nst - 1, True)

    def count_ge_top15(cand15):
        both = (cand15 << 16) | cand15

        def body(st, acc):
            off = pl.multiple_of(st * (KEY_SUPER // 2), KEY_SUPER // 2)
            parts = []
            for t in range(TILES_PER_SUPER // 2):
                diff = pk_s[pl.ds(off + t * KEY_TILE, KEY_TILE), :] - both
                ge = lax.shift_right_logical(diff, 15) & PAIR_ONES
                parts.append(jnp.sum(ge.reshape(KEY_TILE // 8, 8, QB), axis=0))
            return acc + (parts[0] + parts[1])
        acc = lax.fori_loop(0, nst, body, jnp.zeros((8, QB), I32))
        acc = (acc & 0xFFFF) + lax.shift_right_logical(acc, 16)
        return jnp.sum(acc, axis=0, keepdims=True)

    def count_ge(cand):
        def body(st, acc):
            off = pl.multiple_of(st * KEY_SUPER, KEY_SUPER)
            parts = []
            for t in range(TILES_PER_SUPER):
                ge = jnp.where(key_s[pl.ds(off + t * KEY_TILE, KEY_TILE), :] >= cand, 1, 0)
                parts.append(jnp.sum(ge.reshape(KEY_TILE // 8, 8, QB), axis=0))
            return acc + ((parts[0] + parts[1]) + (parts[2] + parts[3]))
        acc = lax.fori_loop(0, nst, body, jnp.zeros((8, QB), I32))
        return jnp.sum(acc, axis=0, keepdims=True)

    def accept(prefix, cnt_ge, cand_u, cnt):
        ok = cnt >= topk
        return jnp.where(ok, cand_u, prefix), jnp.where(ok, cnt, cnt_ge)

    def bit_step(i, carry):
        prefix, cnt_ge = carry
        cand_u = prefix | lax.shift_left(jnp.int32(1), KEY_BITS - 1 - i)
        return accept(prefix, cnt_ge, cand_u, count_ge_top15(lax.shift_right_logical(cand_u, 1)))

    prefix, cnt_ge = lax.fori_loop(
        0, KEY_BITS - 1, bit_step,
        (jnp.zeros((1, QB), I32), jnp.zeros((1, QB), I32) + nst * KEY_SUPER))
    prefix, cnt_ge = accept(prefix, cnt_ge, prefix | 1, count_ge((prefix | 1) + KEY_MIN))
    thresh = prefix + KEY_MIN
    excess = (cnt_ge - topk).astype(F32)

    def mask_tile(st, later_ties, masked):
        off = pl.multiple_of(st * KEY_SUPER, KEY_SUPER)
        for t in reversed(range(TILES_PER_SUPER)):
            o = off + t * KEY_TILE
            key = key_s[pl.ds(o, KEY_TILE), :]
            eq = key == thresh
            eq_f = jnp.where(eq, 1.0, 0.0)
            suffix = later_ties + _dot(ut_ref[...], eq_f.astype(BF16))
            tie_bias = jnp.where(suffix > excess, 0.0, NEG_INF)
            bias = jnp.where(key > thresh, 0.0, jnp.where(eq, tie_bias, NEG_INF))
            if masked:
                bias = jnp.where(o + row < limit, bias, NEG_INF)
            bias_s[pl.ds(o, KEY_TILE), :] = bias
            later_ties = later_ties + jnp.sum(eq_f, axis=0, keepdims=True)
        return later_ties

    ties_in_last = mask_tile(nst - 1, jnp.zeros((1, QB), F32), True)
    lax.fori_loop(0, nst - 1, lambda i, ties: mask_tile(nst - 2 - i, ties, False), ties_in_last)

    acc_s[...] = jnp.zeros(acc_s.shape, F32)

    @pl.when(shift_is_safe)
    def _attend_shifted():
        def scores(st):
            off = pl.multiple_of(st * KEY_SUPER, KEY_SUPER)
            return _dot_nt(k1_s[pl.ds(off, KEY_SUPER), :], qs_s[...])

        def attend(st, sp):
            off = pl.multiple_of(st * KEY_SUPER, KEY_SUPER)
            bias = bias_s[pl.ds(off, KEY_SUPER), :]
            p = jnp.concatenate(
                [jnp.exp2(sp[:, h * QB:(h + 1) * QB] + bias).astype(BF16) for h in range(A_HEADS)],
                axis=1)
            acc_s[...] += _dot(vt_s[st], p)

        def attend_pair(i, _):
            sp0 = scores(2 * i)
            sp1 = scores(2 * i + 1)
            attend(2 * i, sp0)
            attend(2 * i + 1, sp1)
            return 0

        lax.fori_loop(0, nst >> 1, attend_pair, 0)

        @pl.when((nst & 1) == 1)
        def _odd_tail():
            attend(nst - 1, scores(nst - 1))

    @pl.when(jnp.logical_not(shift_is_safe))
    def _attend_online():
        m_s[...] = jnp.full(m_s.shape, NEG_INF, F32)

        def attend_tile(st, _):
            off = pl.multiple_of(st * KEY_SUPER, KEY_SUPER)
            k_tile = k1_s[pl.ds(off, KEY_SUPER), :]
            bias = bias_s[pl.ds(off, KEY_SUPER), :]
            for h in range(A_HEADS):
                cols = slice(h * QB, (h + 1) * QB)
                s = _dot_nt(k_tile, qs_s[cols, :]) + bias
                m_old = m_s[h:h + 1, :]
                m_new = jnp.maximum(m_old, jnp.max(s, axis=0, keepdims=True))
                m_safe = jnp.where(m_new == NEG_INF, 0.0, m_new)
                p = jnp.exp2(s - m_safe).astype(BF16)
                acc_s[:, cols] = jnp.exp2(m_old - m_safe) * acc_s[:, cols] + _dot(vt_s[st], p)
                m_s[h:h + 1, :] = m_new
            return 0

        lax.fori_loop(0, nst, attend_tile, 0)

    out_t = jnp.concatenate(
        [acc_s[0:A_HEAD_DIM, h * QB:(h + 1) * QB] / acc_s[A_HEAD_DIM:A_HEAD_DIM + 1, h * QB:(h + 1) * QB]
         for h in range(A_HEADS)], axis=0)
    o_ref[...] = out_t.T.astype(BF16)


def _dsa(q, iq, ikw, kv, qg, kg, bd, ut, B, T, QB):
    N = q.shape[0]
    nqb = T // QB
    topk = min(TOPK_MAX, T // 4)
    qrow = lambda b, j: (b * nqb + j, 0)
    brow = lambda b, j: (b, 0)
    return pl.pallas_call(
        functools.partial(_dsa_kernel, topk=topk, QB=QB),
        grid=(B, nqb),
        in_specs=[pl.BlockSpec((QB, 512), qrow), pl.BlockSpec((QB, 256), qrow),
                  pl.BlockSpec((QB, 128), qrow), pl.BlockSpec((T, 128), brow),
                  pl.BlockSpec((T, 128), brow), _resident((1, 512)), _resident((1, 128)),
                  _resident((512, 512)), _resident((KEY_TILE, KEY_TILE))],
        out_specs=pl.BlockSpec((QB, 512), qrow),
        out_shape=jax.ShapeDtypeStruct((N, 512), BF16),
        scratch_shapes=[
            pltpu.VMEM((T, 128), BF16),
            pltpu.VMEM((T, 128), BF16),
            pltpu.VMEM((T // KEY_SUPER, V_ROWS, KEY_SUPER), BF16),
            pltpu.VMEM((A_HEADS * QB, 128), BF16),
            pltpu.VMEM((IDX_HEADS * QB, 128), BF16),
            pltpu.VMEM((128, QB), F32),
            pltpu.VMEM((T, QB), I32),
            pltpu.VMEM((T // 2, QB), I32),
            pltpu.VMEM((T, QB), F32),
            pltpu.VMEM((A_HEADS, QB), F32),
            pltpu.VMEM((V_ROWS, A_HEADS * QB), F32),
        ],
        compiler_params=_cparams(2),
        name="dsa_attention",
    )(q, iq, ikw, kv, ikw, qg, kg, bd, ut)


def _gelu_tanh(x):
    return 0.5 * x * (1.0 + jnp.tanh(math.sqrt(2.0 / math.pi) * (x + 0.044715 * (x * x * x))))


def _s5_kernel(u_ref, bm_ref, cm_ref, are_ref, aim_ref, d_ref, wg_ref, bg_ref, o_ref, buf_s, h_s, tm_s,
               *, B, steps):
    n_slab = bm_ref.shape[0]
    half = bm_ref.shape[2] // 2
    width = 2 * half
    n_lane_slabs = tm_s.shape[0]

    @pl.when(pl.program_id(0) == 0)
    def _init():
        h_s[...] = jnp.zeros(h_s.shape, F32)

    for b in range(B):
        for j in range(n_lane_slabs):
            c = b * 128 * n_lane_slabs + 128 * j
            tm_s[j, pl.ds(b, steps, stride=B), :] = u_ref[:, c:c + 128].astype(F32)
    u32 = jnp.concatenate([tm_s[j] for j in range(n_lane_slabs)], axis=1)
    u = u32.astype(BF16)

    def expand(k):
        buf_s[:, width * k:width * (k + 1)] = _dot(u[:, 128 * k:128 * (k + 1)], bm_ref[k])

    ys = []
    expand(0)
    for k in range(n_slab):
        if k + 1 < n_slab:
            expand(k + 1)
        c0 = width * k
        ar = jnp.broadcast_to(are_ref[:, half * k:half * (k + 1)], (B, half))
        ai = jnp.broadcast_to(aim_ref[:, half * k:half * (k + 1)], (B, half))
        hr, hi = h_s[:, c0:c0 + half], h_s[:, c0 + half:c0 + width]
        for t in range(steps):
            rows = slice(t * B, (t + 1) * B)
            hr, hi = (ar * hr - ai * hi + buf_s[rows, c0:c0 + half],
                      ar * hi + ai * hr + buf_s[rows, c0 + half:c0 + width])
            buf_s[rows, c0:c0 + half] = hr
            buf_s[rows, c0 + half:c0 + width] = hi
        h_s[:, c0:c0 + half] = hr
        h_s[:, c0 + half:c0 + width] = hi
        ys.append(_dot(buf_s[:, c0:c0 + width].astype(BF16), cm_ref[k]))
    y = jnp.concatenate(ys, axis=1)
    y = _gelu_tanh(y + d_ref[...] * u32)
    z = _dot(y.astype(BF16), wg_ref[...]) + bg_ref[...]
    out = y * (1.0 / (1.0 + jnp.exp(-z)))
    for j in range(n_lane_slabs):
        tm_s[j] = out[:, 128 * j:128 * (j + 1)]
    for b in range(B):
        for j in range(n_lane_slabs):
            c = b * 128 * n_lane_slabs + 128 * j
            o_ref[:, c:c + 128] = tm_s[j, pl.ds(b, steps, stride=B), :].astype(BF16)


def _s5(u_bt, bm, cm, a_re, a_im, d, wg, bg, B, T, steps):
    rows = steps * B
    width = u_bt.shape[1] // B
    n_state = bm.shape[0] * bm.shape[2]
    return pl.pallas_call(
        functools.partial(_s5_kernel, B=B, steps=steps),
        grid=(T // steps,),
        in_specs=[pl.BlockSpec((steps, B * width), lambda i: (i, 0)), _resident(bm.shape), _resident(cm.shape),
                  _resident(a_re.shape), _resident(a_im.shape), _resident(d.shape),
                  _resident(wg.shape), _resident(bg.shape)],
        out_specs=pl.BlockSpec((steps, B * width), lambda i: (i, 0)),
        out_shape=jax.ShapeDtypeStruct((T, B * width), BF16),
        scratch_shapes=[pltpu.VMEM((rows, n_state), F32),
                        pltpu.VMEM((B, n_state), F32),
                        pltpu.VMEM((width // 128, rows, 128), F32)],
        compiler_params=_cparams(1),
        name="s5_scan",
    )(u_bt, bm, cm, a_re, a_im, d, wg, bg)


def _reset_halo(halo_s, first_tile):
    @pl.when(first_tile)
    def _zero_halo():
        halo_s[0:CONV_HALO, :] = jnp.zeros((CONV_HALO, halo_s.shape[1]), F32)


def _causal_conv3(c, halo_s, w_ref, b_ref, cols):
    tm = c.shape[0]
    halo_s[CONV_HALO:CONV_HALO + tm, cols] = c
    c1 = halo_s[CONV_HALO - 1:CONV_HALO - 1 + tm, cols]
    c2 = halo_s[CONV_HALO - 2:CONV_HALO - 2 + tm, cols]
    y = w_ref[0:1, cols] * c2 + w_ref[1:2, cols] * c1 + w_ref[2:3, cols] * c + b_ref[:, cols]
    halo_s[0:CONV_HALO, cols] = halo_s[tm:tm + CONV_HALO, cols]
    return y


def _odd_kernel(x_ref, g_ref, wi_ref, cw_ref, cb_ref, wo_ref, o_ref, halo_s):
    D = x_ref.shape[1]
    tm = x_ref.shape[0]
    _reset_halo(halo_s, pl.program_id(1) == 0)
    work = [(r0, c0) for r0 in range(0, tm, ROW_SUB) for c0 in range(0, D, ODD_CHUNK)]
    normed = {}

    def in_proj(r0, c0):
        if r0 not in normed:
            normed[r0] = _rms_rows(x_ref[r0:r0 + ROW_SUB, :], g_ref[...]).astype(BF16)
        hn = normed[r0]
        return tuple(_dot(hn, wi_ref[:, part * D + c0:part * D + c0 + ODD_CHUNK]) for part in range(3))

    nxt = in_proj(*work[0])
    acc = None
    for n, (r0, c0) in enumerate(work):
        gb, gc, z = nxt
        if n + 1 < len(work):
            nxt = in_proj(*work[n + 1])
        if c0 == 0:
            acc = x_ref[r0:r0 + ROW_SUB, :]
        cols = slice(c0, c0 + ODD_CHUNK)
        conv = _causal_conv3(gc * z, halo_s, cw_ref, cb_ref, cols)
        acc = acc + _dot((gb * conv).astype(BF16), wo_ref[cols, :])
        if c0 + ODD_CHUNK >= D:
            o_ref[r0:r0 + ROW_SUB, :] = acc


def _odd_mixer(x2, g, wi, cw, cb, wo, B, T, tm):
    N, D = x2.shape
    nt = T // tm
    row = lambda b, t: (b * nt + t, 0)
    return pl.pallas_call(
        _odd_kernel,
        grid=(B, nt),
        in_specs=[pl.BlockSpec((tm, D), row), _resident((1, D)), _resident(wi.shape),
                  _resident(cw.shape), _resident(cb.shape), _resident(wo.shape)],
        out_specs=pl.BlockSpec((tm, D), row),
        out_shape=jax.ShapeDtypeStruct((N, D), F32),
        scratch_shapes=[pltpu.VMEM((ROW_SUB + CONV_HALO, D), F32)],
        compiler_params=_cparams(2),
        name="odd_mixer",
    )(x2, g, wi, cw, cb, wo)


def _mem_kv_kernel(m_ref, g_ref, w_ref, kg_ref, k_ref, v_ref):
    hn = _rms_rows(m_ref[...], g_ref[0])
    y = _dot(hn.astype(BF16), w_ref[0])
    ks = []
    for h in range(X_HEADS):
        ks.append(_rms_rows(y[:, X_HEAD_DIM * h:X_HEAD_DIM * (h + 1)], kg_ref[0]))
    k_ref[0] = jnp.concatenate(ks, axis=1).astype(BF16)
    v_ref[0] = y[:, X_WIDTH:2 * X_WIDTH].astype(BF16)


def _mem_kv(mem2, g, w, kg):
    depth = w.shape[0]
    NM, D = mem2.shape
    return pl.pallas_call(
        _mem_kv_kernel,
        grid=(depth,),
        in_specs=[_resident((NM, D)), pl.BlockSpec((1, 1, D), lambda i: (i, 0, 0)),
                  pl.BlockSpec((1, D, 2 * X_WIDTH), lambda i: (i, 0, 0)),
                  pl.BlockSpec((1, 1, X_HEAD_DIM), lambda i: (i, 0, 0))],
        out_specs=[pl.BlockSpec((1, NM, X_WIDTH), lambda i: (i, 0, 0)),
                   pl.BlockSpec((1, NM, X_WIDTH), lambda i: (i, 0, 0))],
        out_shape=[jax.ShapeDtypeStruct((depth, NM, X_WIDTH), BF16)] * 2,
        compiler_params=_cparams(1),
        name="mem_kv",
    )(mem2, g, w, kg)


def _xattn_kernel(*refs, mixer_out):
    if mixer_out:
        x_ref, ya_ref, yb_ref, wm_ref, g_ref, wq_ref, qg_ref, k_ref, v_ref, wo_ref, o_ref = refs
        na = ya_ref.shape[1]
        x = x_ref[...] + _dot(ya_ref[...], wm_ref[0:na, :]) + _dot(yb_ref[...], wm_ref[na:, :])
    else:
        x_ref, g_ref, wq_ref, qg_ref, k_ref, v_ref, wo_ref, o_ref = refs
        x = x_ref[...]
    hn = _rms_rows(x, g_ref[...])
    q = _dot(hn.astype(BF16), wq_ref[...])
    k = k_ref[0]
    v = v_ref[0]
    heads = [slice(X_HEAD_DIM * h, X_HEAD_DIM * (h + 1)) for h in range(X_HEADS)]
    scores = [_dot_nt((_rms_rows(q[:, sl], qg_ref[...]) * (X_HEAD_DIM ** -0.5)).astype(BF16), k[:, sl])
              for sl in heads]
    outs = []
    for sl, s in zip(heads, scores):
        p = jnp.exp(s - jnp.max(s, axis=-1, keepdims=True))
        outs.append(_dot(p.astype(BF16), v[:, sl]) / jnp.sum(p, axis=-1, keepdims=True))
    o_ref[...] = x + _dot(jnp.concatenate(outs, axis=1).astype(BF16), wo_ref[...])


def _xattn(x2, mixer_out, g, wq, qg, k_all, v_all, wo, layer, B, T, tm, M):
    N, D = x2.shape
    nt = T // tm
    row = lambda b, t: (b * nt + t, 0)
    mrow = lambda b, t: (layer * B + b, 0, 0)
    mixer_args, mixer_specs = (), []
    if mixer_out is not None:
        ya, yb_t, w_mix = mixer_out
        mixer_args = (ya, yb_t, w_mix)
        mixer_specs = [pl.BlockSpec((tm, ya.shape[1]), row),
                       pl.BlockSpec((tm, w_mix.shape[0] - ya.shape[1]), lambda b, t: (t, b)), _resident(w_mix.shape)]
    return pl.pallas_call(
        functools.partial(_xattn_kernel, mixer_out=mixer_out is not None),
        grid=(B, nt),
        in_specs=[pl.BlockSpec((tm, D), row)] + mixer_specs + [
            _resident((1, D)), _resident(wq.shape), _resident((1, X_HEAD_DIM)),
            pl.BlockSpec((1, M, X_WIDTH), mrow), pl.BlockSpec((1, M, X_WIDTH), mrow), _resident(wo.shape)],
        out_specs=pl.BlockSpec((tm, D), row),
        out_shape=jax.ShapeDtypeStruct((N, D), F32),
        compiler_params=_cparams(2),
        name="mem_xattn",
    )(x2, *mixer_args, g, wq, qg, k_all, v_all, wo)


def _ffn_kernel(x_ref, g_ref, wu_ref, cw_ref, cb_ref, wd_ref, o_ref, halo_s):
    F = cw_ref.shape[1]
    tm = x_ref.shape[0]
    _reset_halo(halo_s, pl.program_id(1) == 0)
    work = [(r0, c0) for r0 in range(0, tm, ROW_SUB) for c0 in range(0, F, MLP_CHUNK)]
    normed = {}

    def up_proj(r0, c0):
        if r0 not in normed:
            normed[r0] = _rms_rows(x_ref[r0:r0 + ROW_SUB, :], g_ref[...]).astype(BF16)
        hn = normed[r0]
        return _dot(hn, wu_ref[:, c0:c0 + MLP_CHUNK]), _dot(hn, wu_ref[:, F + c0:F + c0 + MLP_CHUNK])

    nxt = up_proj(*work[0])
    acc = None
    for n, (r0, c0) in enumerate(work):
        gate, up = nxt
        if n + 1 < len(work):
            nxt = up_proj(*work[n + 1])
        if c0 == 0:
            acc = x_ref[r0:r0 + ROW_SUB, :]
        cols = slice(c0, c0 + MLP_CHUNK)
        conv = _causal_conv3(gate, halo_s, cw_ref, cb_ref, cols)
        act = conv * (1.0 / (1.0 + jnp.exp(-conv))) * up
        acc = acc + _dot(act.astype(BF16), wd_ref[cols, :])
        if c0 + MLP_CHUNK >= F:
            o_ref[r0:r0 + ROW_SUB, :] = acc


def _ffn(x2, g, wu, cw, cb, wd, B, T, tm):
    N, D = x2.shape
    F = cw.shape[1]
    nt = T // tm
    row = lambda b, t: (b * nt + t, 0)
    return pl.pallas_call(
        _ffn_kernel,
        grid=(B, nt),
        in_specs=[pl.BlockSpec((tm, D), row), _resident((1, D)),
                  pl.BlockSpec(wu.shape, lambda b, t: (0, 0), pipeline_mode=pl.Buffered(1)),
                  _resident(cw.shape), _resident(cb.shape),
                  pl.BlockSpec(wd.shape, lambda b, t: (0, 0), pipeline_mode=pl.Buffered(1))],
        out_specs=pl.BlockSpec((tm, D), row),
        out_shape=jax.ShapeDtypeStruct((N, D), F32),
        scratch_shapes=[pltpu.VMEM((ROW_SUB + CONV_HALO, F), F32)],
        compiler_params=_cparams(2),
        name="conv_glu_ffn",
    )(x2, g, wu, cw, cb, wd)


def _s5_operators(lam_re, lam_im, log_step, b_re, b_im, c_re, c_im):
    G, P = lam_re.shape
    C = b_re.shape[-1]
    S = S5_GROUPS_PER_SLAB
    n_slab = G // S
    delta = jnp.exp(log_step)[:, None]
    mag = jnp.exp(lam_re * delta)
    bar_re = mag * jnp.cos(lam_im * delta)
    bar_im = mag * jnp.sin(lam_im * delta)
    den = lam_re * lam_re + lam_im * lam_im
    coef_re = ((bar_re - 1.0) * lam_re + bar_im * lam_im) / den
    coef_im = (bar_im * lam_re - (bar_re - 1.0) * lam_im) / den
    bb_re = coef_re[..., None] * b_re - coef_im[..., None] * b_im
    bb_im = coef_re[..., None] * b_im + coef_im[..., None] * b_re
    eye = jnp.eye(S, dtype=F32)

    def in_blocks(m):
        return jnp.einsum('kgpc,gh->kgchp', m.reshape(n_slab, S, P, C), eye).reshape(n_slab, S * C, S * P)

    def out_blocks(m):
        return jnp.einsum('kgcp,gh->kgphc', m.reshape(n_slab, S, C, P), eye).reshape(n_slab, S * P, S * C)

    bm = jnp.concatenate([in_blocks(bb_re), in_blocks(bb_im)], axis=2)
    cm = jnp.concatenate([out_blocks(c_re), -out_blocks(c_im)], axis=1)
    a_re = bar_re.reshape(1, G * P)
    a_im = bar_im.reshape(1, G * P)
    return bm.astype(BF16), cm.astype(BF16), a_re, a_im


def _even_w_in_layout(w):
    n_front = A_WIDTH + 2 * A_HEAD_DIM + IDX_HEADS * IDX_DIM + IDX_DIM + IDX_HEADS
    pad = jnp.zeros((w.shape[0], 128 - IDX_DIM - IDX_HEADS), w.dtype)
    return jnp.concatenate([w[:, :n_front], pad, w[:, n_front:]], axis=1).astype(BF16)


def kernel(x, mem, norm_mix, norm_x, norm_mem, norm_ffn, even_w_in, even_w_out, a_q_norm, a_k_norm,
           s5_lam_re, s5_lam_im, s5_log_step, s5_b_re, s5_b_im, s5_c_re, s5_c_im, s5_d, s5_w_glu, s5_b_glu,
           odd_w_in, odd_conv_w, odd_conv_b, odd_w_out, x_w_q, x_w_kv, x_w_o, x_q_norm, x_k_norm,
           f_w_up, f_conv_w, f_conv_b, f_w_down):
    B, T, D = x.shape
    M = mem.shape[1]
    depth = norm_mix.shape[0]
    assert D == 1024 and T % KEY_SUPER == 0 and even_w_in.shape[2] == 1476
    tm_proj, s5_steps = min(1024, T), 64
    tm_odd = tm_ffn = tm_proj
    assert T % tm_proj == 0 and T % Q_BLOCK == 0

    x2 = x.reshape(B * T, D)
    mem2 = mem.reshape(B * M, D)

    k_all, v_all = _mem_kv(mem2, norm_mem.reshape(depth, 1, D), x_w_kv.astype(BF16),
                           x_k_norm.reshape(depth, 1, X_HEAD_DIM))
    k_all = k_all.reshape(depth * B, M, X_WIDTH)
    v_all = v_all.reshape(depth * B, M, X_WIDTH)

    head_mean = jnp.kron(jnp.eye(A_HEADS, dtype=F32),
                         jnp.full((A_HEAD_DIM, A_HEAD_DIM), 1.0 / A_HEAD_DIM, F32)).astype(BF16)
    upper_tri = jnp.triu(jnp.ones((KEY_TILE, KEY_TILE), F32)).astype(BF16)

    for i in range(depth):
        j = i // 2
        g_mix = norm_mix[i].reshape(1, D)
        if i % 2 == 0:
            q, kv, iq, ikw, u_t = _even_in(x2, g_mix, _even_w_in_layout(even_w_in[j]), B, T, tm_proj)
            qg = jnp.tile(a_q_norm[j], A_HEADS).reshape(1, A_WIDTH)
            kg = jnp.concatenate([a_k_norm[j], jnp.zeros((128 - A_HEAD_DIM,), F32)]).reshape(1, 128)
            ya = _dsa(q, iq, ikw, kv, qg, kg, head_mean, upper_tri, B, T, Q_BLOCK)
            bm, cm, a_re, a_im = _s5_operators(s5_lam_re[j], s5_lam_im[j], s5_log_step[j],
                                               s5_b_re[j], s5_b_im[j], s5_c_re[j], s5_c_im[j])
            yb_t = _s5(u_t, bm, cm, a_re, a_im, s5_d[j].reshape(1, 512),
                        s5_w_glu[j].astype(BF16), s5_b_glu[j].reshape(1, 512), B, T, s5_steps)
            mixer_out = (ya, yb_t, even_w_out[j].astype(BF16))
        else:
            x2 = _odd_mixer(x2, g_mix, odd_w_in[j].astype(BF16), odd_conv_w[j],
                            odd_conv_b[j].reshape(1, D), odd_w_out[j].astype(BF16), B, T, tm_odd)
            mixer_out = None
        x2 = _xattn(x2, mixer_out, norm_x[i].reshape(1, D), x_w_q[i].astype(BF16),
                    x_q_norm[i].reshape(1, X_HEAD_DIM), k_all, v_all, x_w_o[i].astype(BF16), i, B, T, tm_proj, M)
        x2 = _ffn(x2, norm_ffn[i].reshape(1, D), f_w_up[i].astype(BF16), f_conv_w[i],
                  f_conv_b[i].reshape(1, -1), f_w_down[i].astype(BF16), B, T, tm_ffn)
    return x2.reshape(B, T, D)
```

```python
import functools
import math

import jax
import jax.numpy as jnp
from jax import lax
from jax.experimental import pallas as pl
from jax.experimental.pallas import tpu as pltpu

F32 = jnp.float32
BF16 = jnp.bfloat16
I32 = jnp.int32

EPS = 1e-6
CHUNK = 64
Q_BLOCK = 512
KEY_TILE = 128
TILES_PER_SUPER = 4
KEY_SUPER = KEY_TILE * TILES_PER_SUPER
KEY_BITS = 16
KEY_MIN = -(1 << (KEY_BITS - 1))
PAIR_GUARD = -2147450880
PAIR_ONES = 0x00010001
QUAD_ONES = 0x01010101
QUAD_HI_BYTES = -16711936
QUAD_LO_BYTES = 0x00FF00FF
V_ROWS = 80
LOG2_E = 1.4426950408889634
SHIFT_SLACK = 1.02
MAX_SAFE_SHIFT = 50.0
A_HEADS = 8
A_HEAD_DIM = 64
A_WIDTH = A_HEADS * A_HEAD_DIM
IDX_HEADS = 4
IDX_DIM = 64
TOPK_MAX = 256
S5_GROUPS_PER_SLAB = 8
X_HEADS = 4
X_HEAD_DIM = 128
X_WIDTH = X_HEADS * X_HEAD_DIM
CONV_HALO = 8
MLP_CHUNK = 256
ROW_SUB = 256
ODD_CHUNK = 256

V7X_VMEM_LIMIT_BYTES = 56 * 1024 * 1024
NEG_INF = float("-inf")


def _cparams(n_axes):
    return pltpu.CompilerParams(
        dimension_semantics=("arbitrary",) * n_axes,
        vmem_limit_bytes=V7X_VMEM_LIMIT_BYTES)


def _rms_rows(xf, g):
    ms = jnp.mean(xf * xf, axis=-1, keepdims=True)
    return xf * lax.rsqrt(ms + EPS) * g


def _dot(a, b):
    return jnp.dot(a, b, preferred_element_type=F32)


def _dot_nt(a, b):
    return lax.dot_general(a, b, (((1,), (1,)), ((), ())), preferred_element_type=F32)


def _resident(shape):
    nd = len(shape)
    return pl.BlockSpec(shape, lambda *_: (0,) * nd)


def _even_in_kernel(x_ref, g_ref, w_ref, q_ref, kv_ref, iq_ref, ikw_ref, u_ref):
    hn = _rms_rows(x_ref[...], g_ref[...])
    y = _dot(hn.astype(BF16), w_ref[...])
    q_ref[...] = y[:, 0:512].astype(BF16)
    kv_ref[...] = y[:, 512:640].astype(BF16)
    iq_ref[...] = y[:, 640:896].astype(BF16)
    ikw_ref[...] = y[:, 896:1024]
    u_ref[...] = y[:, 1024:1536].astype(BF16)


def _even_in(x2, g, w, B, T, tm):
    N, D = x2.shape
    nt = T // tm
    row = lambda b, t: (b * nt + t, 0)
    return pl.pallas_call(
        _even_in_kernel,
        grid=(B, nt),
        in_specs=[pl.BlockSpec((tm, D), row), _resident((1, D)), _resident(w.shape)],
        out_specs=[pl.BlockSpec((tm, 512), row), pl.BlockSpec((tm, 128), row),
                   pl.BlockSpec((tm, 256), row), pl.BlockSpec((tm, 128), row),
                   pl.BlockSpec((tm, 512), lambda b, t: (t, b))],
        out_shape=[jax.ShapeDtypeStruct((N, 512), BF16), jax.ShapeDtypeStruct((N, 128), BF16),
                   jax.ShapeDtypeStruct((N, 256), BF16), jax.ShapeDtypeStruct((N, 128), F32),
                   jax.ShapeDtypeStruct((T, B * 512), BF16)],
        compiler_params=_cparams(2),
        name="even_in",
    )(x2, g, w)


def _dsa_kernel(q_ref, iq_ref, iwq_ref, kv_ref, ikf_ref, qg_ref, kg_ref, bd_ref, ut_ref, o_ref,
                k1_s, ik2_s, vt_s, qs_s, iqs_s, wt_s, key_s, pk_s, p4_s, bias_s, m_s, acc_s, *, topk, QB):
    T = kv_ref.shape[0]
    qb = pl.program_id(1)
    nst = ((qb + 1) * (QB // KEY_TILE) + TILES_PER_SUPER - 1) >> (TILES_PER_SUPER.bit_length() - 1)

    qlane = lax.broadcasted_iota(I32, (KEY_TILE, QB), 1)
    row = lax.broadcasted_iota(I32, (KEY_TILE, QB), 0)
    lane = lax.broadcasted_iota(I32, (KEY_TILE, 128), 1)
    lo_half = lane < A_HEAD_DIM
    shift_lane = lane == A_HEAD_DIM

    q_scale = (A_HEAD_DIM ** -0.5) * LOG2_E
    shift = (A_HEAD_DIM * q_scale * SHIFT_SLACK) * jnp.max(jnp.abs(qg_ref[...])) * jnp.max(jnp.abs(kg_ref[...]))
    shift_is_safe = shift <= MAX_SAFE_SHIFT

    @pl.when(qb == 0)
    def _prepare_batch():
        ones_rows = jnp.where(lax.broadcasted_iota(I32, (V_ROWS - A_HEAD_DIM, KEY_TILE), 0) == 0, 1.0, 0.0)

        def key_tile(st, _):
            for t in range(TILES_PER_SUPER):
                off = pl.multiple_of(st * KEY_SUPER + t * KEY_TILE, KEY_TILE)
                kv = kv_ref[pl.ds(off, KEY_TILE), :].astype(F32)
                k = jnp.where(lo_half, kv, 0.0)
                ms = jnp.sum(k * k, axis=-1, keepdims=True) * (1.0 / A_HEAD_DIM)
                kn = k * lax.rsqrt(ms + EPS) * kg_ref[...]
                k1_s[pl.ds(off, KEY_TILE), :] = jnp.where(shift_lane, 1.0, kn).astype(BF16)
                ik = jnp.where(lo_half, ikf_ref[pl.ds(off, KEY_TILE), :], 0.0)
                ik2_s[pl.ds(off, KEY_TILE), :] = (ik + pltpu.roll(ik, 64, 1)).astype(BF16)
                v = pltpu.roll(jnp.where(lo_half, 0.0, kv), 64, 1)
                vt1 = jnp.concatenate([v.T[0:A_HEAD_DIM, :], ones_rows], axis=0)
                vt_s[st, :, t * KEY_TILE:(t + 1) * KEY_TILE] = vt1.astype(BF16)
            return 0

        lax.fori_loop(0, T // KEY_SUPER, key_tile, 0)

    q = q_ref[...].astype(F32)
    ms = _dot((q * q).astype(BF16), bd_ref[...])
    qn = q * lax.rsqrt(ms + EPS) * qg_ref[...] * q_scale
    half_q = lax.broadcasted_iota(I32, (QB, 128), 1) < A_HEAD_DIM
    neg_shift_lane = jnp.where(lax.broadcasted_iota(I32, (QB, 128), 1) == A_HEAD_DIM, -1.0, 0.0) * shift
    for h in range(A_HEADS):
        blk = qn[:, 128 * (h // 2):128 * (h // 2) + 128]
        if h % 2 == 1:
            blk = pltpu.roll(blk, 64, 1)
        qs_s[h * QB:(h + 1) * QB, :] = (jnp.where(half_q, blk, 0.0) + neg_shift_lane).astype(BF16)
    iq = iq_ref[...].astype(F32)
    for h in range(IDX_HEADS):
        blk = iq[:, 128 * (h // 2):128 * (h // 2) + 128]
        keep = half_q if h % 2 == 0 else jnp.logical_not(half_q)
        iqs_s[h * QB:(h + 1) * QB, :] = jnp.where(keep, blk, 0.0).astype(BF16)
    wt_s[...] = iwq_ref[...].T * ((IDX_DIM ** -0.5) * (IDX_HEADS ** -0.5))

    chunk_shift = CHUNK.bit_length() - 1
    limit = (((qb * QB + qlane) >> chunk_shift) + 1) << chunk_shift

    def score_tile(st, masked):
        off = pl.multiple_of(st * KEY_SUPER, KEY_SUPER)
        x = _dot_nt(ik2_s[pl.ds(off, KEY_SUPER), :], iqs_s[...])
        for t in range(TILES_PER_SUPER):
            rows = slice(t * KEY_TILE, (t + 1) * KEY_TILE)
            sc = wt_s[64:65, :] * jnp.maximum(x[rows, 0:QB], 0.0)
            for h in range(1, IDX_HEADS):
                sc = sc + wt_s[64 + h:65 + h, :] * jnp.maximum(x[rows, QB * h:QB * (h + 1)], 0.0)
            sc = jnp.where(sc == 0.0, 0.0, sc)
            if masked:
                sc = jnp.where(off + t * KEY_TILE + row < limit, sc, NEG_INF)
            bits = pltpu.bitcast(sc.astype(BF16).astype(F32), I32) >> 16
            key = bits ^ ((bits >> 31) & 0x7FFF)
            key_s[pl.ds(off + t * KEY_TILE, KEY_TILE), :] = key
            top15 = lax.shift_right_logical(key - KEY_MIN, 1)
            if t % 2 == 0:
                pair_hi = top15 << 16
            else:
                half_off = pl.multiple_of(st * (KEY_SUPER // 2), KEY_SUPER // 2)
                pair = pair_hi | top15 | PAIR_GUARD
                pk_s[pl.ds(half_off + (t // 2) * KEY_TILE, KEY_TILE), :] = pair
                if t == 1:
                    quad_hi = pair & QUAD_HI_BYTES
                else:
                    quarter_off = pl.multiple_of(st * KEY_TILE, KEY_TILE)
                    p4_s[pl.ds(quarter_off, KEY_TILE), :] = quad_hi | (lax.shift_right_logical(pair, 8) & QUAD_LO_BYTES)

    def score_full_tile(st, carry):
        score_tile(st, False)
        return carry

    lax.fori_loop(0, nst - 1, score_full_tile, 0)
    score_tile(nst - 1, True)

    def count_ge_top15(cand15):
        both = (cand15 << 16) | cand15

        def body(st, acc):
            off = pl.multiple_of(st * (KEY_SUPER // 2), KEY_SUPER // 2)
            parts = []
            for t in range(TILES_PER_SUPER // 2):
                diff = pk_s[pl.ds(off + t * KEY_TILE, KEY_TILE), :] - both
                ge = lax.shift_right_logical(diff, 15) & PAIR_ONES
                parts.append(jnp.sum(ge.reshape(KEY_TILE // 8, 8, QB), axis=0))
            return acc + (parts[0] + parts[1])
        acc = lax.fori_loop(0, nst, body, jnp.zeros((8, QB), I32))
        acc = (acc & 0xFFFF) + lax.shift_right_logical(acc, 16)
        return jnp.sum(acc, axis=0, keepdims=True)

    def count_ge_top7(cand7):
        every = cand7 * QUAD_ONES

        def body(st, acc):
            off = pl.multiple_of(st * KEY_TILE, KEY_TILE)
            diff = p4_s[pl.ds(off, KEY_TILE), :] - every
            ge = lax.shift_right_logical(diff, 7) & QUAD_ONES
            return acc + jnp.sum(ge.reshape(KEY_TILE // 8, 8, QB), axis=0)
        acc = lax.fori_loop(0, nst, body, jnp.zeros((8, QB), I32))
        acc = ((acc & 0xFF) + (lax.shift_right_logical(acc, 8) & 0xFF)
               + (lax.shift_right_logical(acc, 16) & 0xFF) + lax.shift_right_logical(acc, 24))
        return jnp.sum(acc, axis=0, keepdims=True)

    def count_ge(cand):
        def body(st, acc):
            off = pl.multiple_of(st * KEY_SUPER, KEY_SUPER)
            parts = []
            for t in range(TILES_PER_SUPER):
                ge = jnp.where(key_s[pl.ds(off + t * KEY_TILE, KEY_TILE), :] >= cand, 1, 0)
                parts.append(jnp.sum(ge.reshape(KEY_TILE // 8, 8, QB), axis=0))
            return acc + ((parts[0] + parts[1]) + (parts[2] + parts[3]))
        acc = lax.fori_loop(0, nst, body, jnp.zeros((8, QB), I32))
        return jnp.sum(acc, axis=0, keepdims=True)

    def accept(prefix, cnt_ge, cand_u, cnt):
        ok = cnt >= topk
        return jnp.where(ok, cand_u, prefix), jnp.where(ok, cnt, cnt_ge)

    def bit_step_top7(i, carry):
        prefix, cnt_ge = carry
        cand_u = prefix | lax.shift_left(jnp.int32(1), KEY_BITS - 1 - i)
        return accept(prefix, cnt_ge, cand_u, count_ge_top7(lax.shift_right_logical(cand_u, KEY_BITS - 7)))

    def bit_step_top15(i, carry):
        prefix, cnt_ge = carry
        cand_u = prefix | lax.shift_left(jnp.int32(1), KEY_BITS - 1 - i)
        return accept(prefix, cnt_ge, cand_u, count_ge_top15(lax.shift_right_logical(cand_u, 1)))

    carry = (jnp.zeros((1, QB), I32), jnp.zeros((1, QB), I32) + nst * KEY_SUPER)
    carry = lax.fori_loop(0, 7, bit_step_top7, carry)
    prefix, cnt_ge = lax.fori_loop(7, KEY_BITS - 1, bit_step_top15, carry)
    prefix, cnt_ge = accept(prefix, cnt_ge, prefix | 1, count_ge((prefix | 1) + KEY_MIN))
    thresh = prefix + KEY_MIN
    excess = (cnt_ge - topk).astype(F32)

    def mask_tile(st, later_ties, masked):
        off = pl.multiple_of(st * KEY_SUPER, KEY_SUPER)
        for t in reversed(range(TILES_PER_SUPER)):
            o = off + t * KEY_TILE
            key = key_s[pl.ds(o, KEY_TILE), :]
            eq = key == thresh
            eq_f = jnp.where(eq, 1.0, 0.0)
            suffix = later_ties + _dot(ut_ref[...], eq_f.astype(BF16))
            tie_bias = jnp.where(suffix > excess, 0.0, NEG_INF)
            bias = jnp.where(key > thresh, 0.0, jnp.where(eq, tie_bias, NEG_INF))
            if masked:
                bias = jnp.where(o + row < limit, bias, NEG_INF)
            bias_s[pl.ds(o, KEY_TILE), :] = bias
            later_ties = later_ties + jnp.sum(eq_f, axis=0, keepdims=True)
        return later_ties

    ties_in_last = mask_tile(nst - 1, jnp.zeros((1, QB), F32), True)
    lax.fori_loop(0, nst - 1, lambda i, ties: mask_tile(nst - 2 - i, ties, False), ties_in_last)

    acc_s[...] = jnp.zeros(acc_s.shape, F32)

    @pl.when(shift_is_safe)
    def _attend_shifted():
        def scores(st):
            off = pl.multiple_of(st * KEY_SUPER, KEY_SUPER)
            return _dot_nt(k1_s[pl.ds(off, KEY_SUPER), :], qs_s[...])

        def attend(st, sp):
            off = pl.multiple_of(st * KEY_SUPER, KEY_SUPER)
            bias = bias_s[pl.ds(off, KEY_SUPER), :]
            p = jnp.concatenate(
                [jnp.exp2(sp[:, h * QB:(h + 1) * QB] + bias).astype(BF16) for h in range(A_HEADS)],
                axis=1)
            acc_s[...] += _dot(vt_s[st], p)

        def attend_pair(i, _):
            sp0 = scores(2 * i)
            sp1 = scores(2 * i + 1)
            attend(2 * i, sp0)
            attend(2 * i + 1, sp1)
            return 0

        lax.fori_loop(0, nst >> 1, attend_pair, 0)

        @pl.when((nst & 1) == 1)
        def _odd_tail():
            attend(nst - 1, scores(nst - 1))

    @pl.when(jnp.logical_not(shift_is_safe))
    def _attend_online():
        m_s[...] = jnp.full(m_s.shape, NEG_INF, F32)

        def attend_tile(st, _):
            off = pl.multiple_of(st * KEY_SUPER, KEY_SUPER)
            k_tile = k1_s[pl.ds(off, KEY_SUPER), :]
            bias = bias_s[pl.ds(off, KEY_SUPER), :]
            for h in range(A_HEADS):
                cols = slice(h * QB, (h + 1) * QB)
                s = _dot_nt(k_tile, qs_s[cols, :]) + bias
                m_old = m_s[h:h + 1, :]
                m_new = jnp.maximum(m_old, jnp.max(s, axis=0, keepdims=True))
                m_safe = jnp.where(m_new == NEG_INF, 0.0, m_new)
                p = jnp.exp2(s - m_safe).astype(BF16)
                acc_s[:, cols] = jnp.exp2(m_old - m_safe) * acc_s[:, cols] + _dot(vt_s[st], p)
                m_s[h:h + 1, :] = m_new
            return 0

        lax.fori_loop(0, nst, attend_tile, 0)

    out_t = jnp.concatenate(
        [acc_s[0:A_HEAD_DIM, h * QB:(h + 1) * QB] / acc_s[A_HEAD_DIM:A_HEAD_DIM + 1, h * QB:(h + 1) * QB]
         for h in range(A_HEADS)], axis=0)
    o_ref[...] = out_t.T.astype(BF16)


def _dsa(q, iq, ikw, kv, qg, kg, bd, ut, B, T, QB):
    N = q.shape[0]
    nqb = T // QB
    topk = min(TOPK_MAX, T // 4)
    qrow = lambda b, j: (b * nqb + j, 0)
    brow = lambda b, j: (b, 0)
    return pl.pallas_call(
        functools.partial(_dsa_kernel, topk=topk, QB=QB),
        grid=(B, nqb),
        in_specs=[pl.BlockSpec((QB, 512), qrow), pl.BlockSpec((QB, 256), qrow),
                  pl.BlockSpec((QB, 128), qrow), pl.BlockSpec((T, 128), brow),
                  pl.BlockSpec((T, 128), brow), _resident((1, 512)), _resident((1, 128)),
                  _resident((512, 512)), _resident((KEY_TILE, KEY_TILE))],
        out_specs=pl.BlockSpec((QB, 512), qrow),
        out_shape=jax.ShapeDtypeStruct((N, 512), BF16),
        scratch_shapes=[
            pltpu.VMEM((T, 128), BF16),
            pltpu.VMEM((T, 128), BF16),
            pltpu.VMEM((T // KEY_SUPER, V_ROWS, KEY_SUPER), BF16),
            pltpu.VMEM((A_HEADS * QB, 128), BF16),
            pltpu.VMEM((IDX_HEADS * QB, 128), BF16),
            pltpu.VMEM((128, QB), F32),
            pltpu.VMEM((T, QB), I32),
            pltpu.VMEM((T // 2, QB), I32),
            pltpu.VMEM((T // 4, QB), I32),
            pltpu.VMEM((T, QB), F32),
            pltpu.VMEM((A_HEADS, QB), F32),
            pltpu.VMEM((V_ROWS, A_HEADS * QB), F32),
        ],
        compiler_params=_cparams(2),
        name="dsa_attention",
    )(q, iq, ikw, kv, ikw, qg, kg, bd, ut)


def _gelu_tanh(x):
    return 0.5 * x * (1.0 + jnp.tanh(math.sqrt(2.0 / math.pi) * (x + 0.044715 * (x * x * x))))


def _s5_kernel(u_ref, bm_ref, cm_ref, are_ref, aim_ref, d_ref, wg_ref, bg_ref, o_ref, buf_s, h_s, tm_s,
               *, B, steps):
    n_slab = bm_ref.shape[0]
    half = bm_ref.shape[2] // 2
    width = 2 * half
    n_lane_slabs = tm_s.shape[0]

    @pl.when(pl.program_id(0) == 0)
    def _init():
        h_s[...] = jnp.zeros(h_s.shape, F32)

    for b in range(B):
        for j in range(n_lane_slabs):
            c = b * 128 * n_lane_slabs + 128 * j
            tm_s[j, pl.ds(b, steps, stride=B), :] = u_ref[:, c:c + 128].astype(F32)
    u32 = jnp.concatenate([tm_s[j] for j in range(n_lane_slabs)], axis=1)
    u = u32.astype(BF16)

    def expand(k):
        buf_s[:, width * k:width * (k + 1)] = _dot(u[:, 128 * k:128 * (k + 1)], bm_ref[k])

    ys = []
    expand(0)
    for k in range(n_slab):
        if k + 1 < n_slab:
            expand(k + 1)
        c0 = width * k
        ar = jnp.broadcast_to(are_ref[:, half * k:half * (k + 1)], (B, half))
        ai = jnp.broadcast_to(aim_ref[:, half * k:half * (k + 1)], (B, half))
        hr, hi = h_s[:, c0:c0 + half], h_s[:, c0 + half:c0 + width]
        for t in range(steps):
            rows = slice(t * B, (t + 1) * B)
            hr, hi = (ar * hr - ai * hi + buf_s[rows, c0:c0 + half],
                      ar * hi + ai * hr + buf_s[rows, c0 + half:c0 + width])
            buf_s[rows, c0:c0 + half] = hr
            buf_s[rows, c0 + half:c0 + width] = hi
        h_s[:, c0:c0 + half] = hr
        h_s[:, c0 + half:c0 + width] = hi
        ys.append(_dot(buf_s[:, c0:c0 + width].astype(BF16), cm_ref[k]))
    y = jnp.concatenate(ys, axis=1)
    y = _gelu_tanh(y + d_ref[...] * u32)
    z = _dot(y.astype(BF16), wg_ref[...]) + bg_ref[...]
    out = y * (1.0 / (1.0 + jnp.exp(-z)))
    for j in range(n_lane_slabs):
        tm_s[j] = out[:, 128 * j:128 * (j + 1)]
    for b in range(B):
        for j in range(n_lane_slabs):
            c = b * 128 * n_lane_slabs + 128 * j
            o_ref[:, c:c + 128] = tm_s[j, pl.ds(b, steps, stride=B), :].astype(BF16)


def _s5(u_bt, bm, cm, a_re, a_im, d, wg, bg, B, T, steps):
    rows = steps * B
    width = u_bt.shape[1] // B
    n_state = bm.shape[0] * bm.shape[2]
    return pl.pallas_call(
        functools.partial(_s5_kernel, B=B, steps=steps),
        grid=(T // steps,),
        in_specs=[pl.BlockSpec((steps, B * width), lambda i: (i, 0)), _resident(bm.shape), _resident(cm.shape),
                  _resident(a_re.shape), _resident(a_im.shape), _resident(d.shape),
                  _resident(wg.shape), _resident(bg.shape)],
        out_specs=pl.BlockSpec((steps, B * width), lambda i: (i, 0)),
        out_shape=jax.ShapeDtypeStruct((T, B * width), BF16),
        scratch_shapes=[pltpu.VMEM((rows, n_state), F32),
                        pltpu.VMEM((B, n_state), F32),
                        pltpu.VMEM((width // 128, rows, 128), F32)],
        compiler_params=_cparams(1),
        name="s5_scan",
    )(u_bt, bm, cm, a_re, a_im, d, wg, bg)


def _reset_halo(halo_s, first_tile):
    @pl.when(first_tile)
    def _zero_halo():
        halo_s[0:CONV_HALO, :] = jnp.zeros((CONV_HALO, halo_s.shape[1]), F32)


def _causal_conv3(c, halo_s, w_ref, b_ref, cols):
    tm = c.shape[0]
    halo_s[CONV_HALO:CONV_HALO + tm, cols] = c
    c1 = halo_s[CONV_HALO - 1:CONV_HALO - 1 + tm, cols]
    c2 = halo_s[CONV_HALO - 2:CONV_HALO - 2 + tm, cols]
    y = w_ref[0:1, cols] * c2 + w_ref[1:2, cols] * c1 + w_ref[2:3, cols] * c + b_ref[:, cols]
    halo_s[0:CONV_HALO, cols] = halo_s[tm:tm + CONV_HALO, cols]
    return y


def _odd_kernel(x_ref, g_ref, wi_ref, cw_ref, cb_ref, wo_ref, o_ref, halo_s):
    D = x_ref.shape[1]
    tm = x_ref.shape[0]
    _reset_halo(halo_s, pl.program_id(1) == 0)
    work = [(r0, c0) for r0 in range(0, tm, ROW_SUB) for c0 in range(0, D, ODD_CHUNK)]
    normed = {}

    def in_proj(r0, c0):
        if r0 not in normed:
            normed[r0] = _rms_rows(x_ref[r0:r0 + ROW_SUB, :], g_ref[...]).astype(BF16)
        hn = normed[r0]
        return tuple(_dot(hn, wi_ref[:, part * D + c0:part * D + c0 + ODD_CHUNK]) for part in range(3))

    nxt = in_proj(*work[0])
    acc = None
    for n, (r0, c0) in enumerate(work):
        gb, gc, z = nxt
        if n + 1 < len(work):
            nxt = in_proj(*work[n + 1])
        if c0 == 0:
            acc = x_ref[r0:r0 + ROW_SUB, :]
        cols = slice(c0, c0 + ODD_CHUNK)
        conv = _causal_conv3(gc * z, halo_s, cw_ref, cb_ref, cols)
        acc = acc + _dot((gb * conv).astype(BF16), wo_ref[cols, :])
        if c0 + ODD_CHUNK >= D:
            o_ref[r0:r0 + ROW_SUB, :] = acc


def _odd_mixer(x2, g, wi, cw, cb, wo, B, T, tm):
    N, D = x2.shape
    nt = T // tm
    row = lambda b, t: (b * nt + t, 0)
    return pl.pallas_call(
        _odd_kernel,
        grid=(B, nt),
        in_specs=[pl.BlockSpec((tm, D), row), _resident((1, D)), _resident(wi.shape),
                  _resident(cw.shape), _resident(cb.shape), _resident(wo.shape)],
        out_specs=pl.BlockSpec((tm, D), row),
        out_shape=jax.ShapeDtypeStruct((N, D), F32),
        scratch_shapes=[pltpu.VMEM((ROW_SUB + CONV_HALO, D), F32)],
        compiler_params=_cparams(2),
        name="odd_mixer",
    )(x2, g, wi, cw, cb, wo)


def _mem_kv_kernel(m_ref, g_ref, w_ref, kg_ref, k_ref, v_ref):
    hn = _rms_rows(m_ref[...], g_ref[0])
    y = _dot(hn.astype(BF16), w_ref[0])
    ks = []
    for h in range(X_HEADS):
        ks.append(_rms_rows(y[:, X_HEAD_DIM * h:X_HEAD_DIM * (h + 1)], kg_ref[0]))
    k_ref[0] = jnp.concatenate(ks, axis=1).astype(BF16)
    v_ref[0] = y[:, X_WIDTH:2 * X_WIDTH].astype(BF16)


def _mem_kv(mem2, g, w, kg):
    depth = w.shape[0]
    NM, D = mem2.shape
    return pl.pallas_call(
        _mem_kv_kernel,
        grid=(depth,),
        in_specs=[_resident((NM, D)), pl.BlockSpec((1, 1, D), lambda i: (i, 0, 0)),
                  pl.BlockSpec((1, D, 2 * X_WIDTH), lambda i: (i, 0, 0)),
                  pl.BlockSpec((1, 1, X_HEAD_DIM), lambda i: (i, 0, 0))],
        out_specs=[pl.BlockSpec((1, NM, X_WIDTH), lambda i: (i, 0, 0)),
                   pl.BlockSpec((1, NM, X_WIDTH), lambda i: (i, 0, 0))],
        out_shape=[jax.ShapeDtypeStruct((depth, NM, X_WIDTH), BF16)] * 2,
        compiler_params=_cparams(1),
        name="mem_kv",
    )(mem2, g, w, kg)


def _xattn_kernel(*refs, mixer_out):
    if mixer_out:
        x_ref, ya_ref, yb_ref, wm_ref, g_ref, wq_ref, qg_ref, k_ref, v_ref, wo_ref, o_ref = refs
        na = ya_ref.shape[1]
        x = x_ref[...] + _dot(ya_ref[...], wm_ref[0:na, :]) + _dot(yb_ref[...], wm_ref[na:, :])
    else:
        x_ref, g_ref, wq_ref, qg_ref, k_ref, v_ref, wo_ref, o_ref = refs
        x = x_ref[...]
    hn = _rms_rows(x, g_ref[...])
    q = _dot(hn.astype(BF16), wq_ref[...])
    k = k_ref[0]
    v = v_ref[0]
    heads = [slice(X_HEAD_DIM * h, X_HEAD_DIM * (h + 1)) for h in range(X_HEADS)]
    scores = [_dot_nt((_rms_rows(q[:, sl], qg_ref[...]) * (X_HEAD_DIM ** -0.5)).astype(BF16), k[:, sl])
              for sl in heads]
    outs = []
    for sl, s in zip(heads, scores):
        p = jnp.exp(s - jnp.max(s, axis=-1, keepdims=True))
        outs.append(_dot(p.astype(BF16), v[:, sl]) / jnp.sum(p, axis=-1, keepdims=True))
    o_ref[...] = x + _dot(jnp.concatenate(outs, axis=1).astype(BF16), wo_ref[...])


def _xattn(x2, mixer_out, g, wq, qg, k_all, v_all, wo, layer, B, T, tm, M):
    N, D = x2.shape
    nt = T // tm
    row = lambda b, t: (b * nt + t, 0)
    mrow = lambda b, t: (layer * B + b, 0, 0)
    mixer_args, mixer_specs = (), []
    if mixer_out is not None:
        ya, yb_t, w_mix = mixer_out
        mixer_args = (ya, yb_t, w_mix)
        mixer_specs = [pl.BlockSpec((tm, ya.shape[1]), row),
                       pl.BlockSpec((tm, w_mix.shape[0] - ya.shape[1]), lambda b, t: (t, b)), _resident(w_mix.shape)]
    return pl.pallas_call(
        functools.partial(_xattn_kernel, mixer_out=mixer_out is not None),
        grid=(B, nt),
        in_specs=[pl.BlockSpec((tm, D), row)] + mixer_specs + [
            _resident((1, D)), _resident(wq.shape), _resident((1, X_HEAD_DIM)),
            pl.BlockSpec((1, M, X_WIDTH), mrow), pl.BlockSpec((1, M, X_WIDTH), mrow), _resident(wo.shape)],
        out_specs=pl.BlockSpec((tm, D), row),
        out_shape=jax.ShapeDtypeStruct((N, D), F32),
        compiler_params=_cparams(2),
        name="mem_xattn",
    )(x2, *mixer_args, g, wq, qg, k_all, v_all, wo)


def _ffn_kernel(x_ref, g_ref, wu_ref, cw_ref, cb_ref, wd_ref, o_ref, halo_s):
    F = cw_ref.shape[1]
    tm = x_ref.shape[0]
    _reset_halo(halo_s, pl.program_id(1) == 0)
    work = [(r0, c0) for r0 in range(0, tm, ROW_SUB) for c0 in range(0, F, MLP_CHUNK)]
    normed = {}

    def up_proj(r0, c0):
        if r0 not in normed:
            normed[r0] = _rms_rows(x_ref[r0:r0 + ROW_SUB, :], g_ref[...]).astype(BF16)
        hn = normed[r0]
        return _dot(hn, wu_ref[:, c0:c0 + MLP_CHUNK]), _dot(hn, wu_ref[:, F + c0:F + c0 + MLP_CHUNK])

    nxt = up_proj(*work[0])
    acc = None
    for n, (r0, c0) in enumerate(work):
        gate, up = nxt
        if n + 1 < len(work):
            nxt = up_proj(*work[n + 1])
        if c0 == 0:
            acc = x_ref[r0:r0 + ROW_SUB, :]
        cols = slice(c0, c0 + MLP_CHUNK)
        conv = _causal_conv3(gate, halo_s, cw_ref, cb_ref, cols)
        act = conv * (1.0 / (1.0 + jnp.exp(-conv))) * up
        acc = acc + _dot(act.astype(BF16), wd_ref[cols, :])
        if c0 + MLP_CHUNK >= F:
            o_ref[r0:r0 + ROW_SUB, :] = acc


def _ffn(x2, g, wu, cw, cb, wd, B, T, tm):
    N, D = x2.shape
    F = cw.shape[1]
    nt = T // tm
    row = lambda b, t: (b * nt + t, 0)
    return pl.pallas_call(
        _ffn_kernel,
        grid=(B, nt),
        in_specs=[pl.BlockSpec((tm, D), row), _resident((1, D)),
                  pl.BlockSpec(wu.shape, lambda b, t: (0, 0), pipeline_mode=pl.Buffered(1)),
                  _resident(cw.shape), _resident(cb.shape),
                  pl.BlockSpec(wd.shape, lambda b, t: (0, 0), pipeline_mode=pl.Buffered(1))],
        out_specs=pl.BlockSpec((tm, D), row),
        out_shape=jax.ShapeDtypeStruct((N, D), F32),
        scratch_shapes=[pltpu.VMEM((ROW_SUB + CONV_HALO, F), F32)],
        compiler_params=_cparams(2),
        name="conv_glu_ffn",
    )(x2, g, wu, cw, cb, wd)


def _s5_operators(lam_re, lam_im, log_step, b_re, b_im, c_re, c_im):
    G, P = lam_re.shape
    C = b_re.shape[-1]
    S = S5_GROUPS_PER_SLAB
    n_slab = G // S
    delta = jnp.exp(log_step)[:, None]
    mag = jnp.exp(lam_re * delta)
    bar_re = mag * jnp.cos(lam_im * delta)
    bar_im = mag * jnp.sin(lam_im * delta)
    den = lam_re * lam_re + lam_im * lam_im
    coef_re = ((bar_re - 1.0) * lam_re + bar_im * lam_im) / den
    coef_im = (bar_im * lam_re - (bar_re - 1.0) * lam_im) / den
    bb_re = coef_re[..., None] * b_re - coef_im[..., None] * b_im
    bb_im = coef_re[..., None] * b_im + coef_im[..., None] * b_re
    eye = jnp.eye(S, dtype=F32)

    def in_blocks(m):
        return jnp.einsum('kgpc,gh->kgchp', m.reshape(n_slab, S, P, C), eye).reshape(n_slab, S * C, S * P)

    def out_blocks(m):
        return jnp.einsum('kgcp,gh->kgphc', m.reshape(n_slab, S, C, P), eye).reshape(n_slab, S * P, S * C)

    bm = jnp.concatenate([in_blocks(bb_re), in_blocks(bb_im)], axis=2)
    cm = jnp.concatenate([out_blocks(c_re), -out_blocks(c_im)], axis=1)
    a_re = bar_re.reshape(1, G * P)
    a_im = bar_im.reshape(1, G * P)
    return bm.astype(BF16), cm.astype(BF16), a_re, a_im


def _even_w_in_layout(w):
    n_front = A_WIDTH + 2 * A_HEAD_DIM + IDX_HEADS * IDX_DIM + IDX_DIM + IDX_HEADS
    pad = jnp.zeros((w.shape[0], 128 - IDX_DIM - IDX_HEADS), w.dtype)
    return jnp.concatenate([w[:, :n_front], pad, w[:, n_front:]], axis=1).astype(BF16)


def kernel(x, mem, norm_mix, norm_x, norm_mem, norm_ffn, even_w_in, even_w_out, a_q_norm, a_k_norm,
           s5_lam_re, s5_lam_im, s5_log_step, s5_b_re, s5_b_im, s5_c_re, s5_c_im, s5_d, s5_w_glu, s5_b_glu,
           odd_w_in, odd_conv_w, odd_conv_b, odd_w_out, x_w_q, x_w_kv, x_w_o, x_q_norm, x_k_norm,
           f_w_up, f_conv_w, f_conv_b, f_w_down):
    B, T, D = x.shape
    M = mem.shape[1]
    depth = norm_mix.shape[0]
    assert D == 1024 and T % KEY_SUPER == 0 and even_w_in.shape[2] == 1476
    tm_proj, s5_steps = min(1024, T), 64
    tm_odd = tm_ffn = tm_proj
    assert T % tm_proj == 0 and T % Q_BLOCK == 0

    x2 = x.reshape(B * T, D)
    mem2 = mem.reshape(B * M, D)

    k_all, v_all = _mem_kv(mem2, norm_mem.reshape(depth, 1, D), x_w_kv.astype(BF16),
                           x_k_norm.reshape(depth, 1, X_HEAD_DIM))
    k_all = k_all.reshape(depth * B, M, X_WIDTH)
    v_all = v_all.reshape(depth * B, M, X_WIDTH)

    head_mean = jnp.kron(jnp.eye(A_HEADS, dtype=F32),
                         jnp.full((A_HEAD_DIM, A_HEAD_DIM), 1.0 / A_HEAD_DIM, F32)).astype(BF16)
    upper_tri = jnp.triu(jnp.ones((KEY_TILE, KEY_TILE), F32)).astype(BF16)

    for i in range(depth):
        j = i // 2
        g_mix = norm_mix[i].reshape(1, D)
        if i % 2 == 0:
            q, kv, iq, ikw, u_t = _even_in(x2, g_mix, _even_w_in_layout(even_w_in[j]), B, T, tm_proj)
            qg = jnp.tile(a_q_norm[j], A_HEADS).reshape(1, A_WIDTH)
            kg = jnp.concatenate([a_k_norm[j], jnp.zeros((128 - A_HEAD_DIM,), F32)]).reshape(1, 128)
            ya = _dsa(q, iq, ikw, kv, qg, kg, head_mean, upper_tri, B, T, Q_BLOCK)
            bm, cm, a_re, a_im = _s5_operators(s5_lam_re[j], s5_lam_im[j], s5_log_step[j],
                                               s5_b_re[j], s5_b_im[j], s5_c_re[j], s5_c_im[j])
            yb_t = _s5(u_t, bm, cm, a_re, a_im, s5_d[j].reshape(1, 512),
                        s5_w_glu[j].astype(BF16), s5_b_glu[j].reshape(1, 512), B, T, s5_steps)
            mixer_out = (ya, yb_t, even_w_out[j].astype(BF16))
        else:
            x2 = _odd_mixer(x2, g_mix, odd_w_in[j].astype(BF16), odd_conv_w[j],
                            odd_conv_b[j].reshape(1, D), odd_w_out[j].astype(BF16), B, T, tm_odd)
            mixer_out = None
        x2 = _xattn(x2, mixer_out, norm_x[i].reshape(1, D), x_w_q[i].astype(BF16),
                    x_q_norm[i].reshape(1, X_HEAD_DIM), k_all, v_all, x_w_o[i].astype(BF16), i, B, T, tm_proj, M)
        x2 = _ffn(x2, norm_ffn[i].reshape(1, D), f_w_up[i].astype(BF16), f_conv_w[i],
                  f_conv_b[i].reshape(1, -1), f_w_down[i].astype(BF16), B, T, tm_ffn)
    return x2.reshape(B, T, D)
```
